```python
import math
import jax, jax.numpy as jnp
from jax import lax
import numpy as np

D_MODEL = 2048
BATCH = 16
SEQ = 256
DEPTH = 2
DEC_BATCH = 8
DEC_SEQ = 1024
PAST_LEN = 512

GRID_W = 64
A_HD = 64
A_HEADS = D_MODEL // (4 * A_HD)
A_QK = A_HEADS * 2 * A_HD
A_V = A_HEADS * 2 * A_HD
POOL_WIDTH = D_MODEL // 2
POOL_WINDOWS = (2, 4, 8, 16)
N_POOL = len(POOL_WINDOWS)
POOL_C = POOL_WIDTH // N_POOL
IN_EVEN = 2 * A_QK + A_V + POOL_WIDTH
MIX_EVEN = A_V + POOL_WIDTH
CONV_C = D_MODEL
CONV_K = 31
PEER_HEADS = 8
PEER_NK = 128
PEER_N = PEER_NK * PEER_NK
PEER_DK = 256
PEER_TOPK = 16
ROPE_THETA = 10000.0
ROPE_AXIS_F = A_HD // 4
Q_BLOCK = 128
TOKEN_BLOCK = 128
N_EVEN = (DEPTH + 1) // 2
N_ODD = DEPTH // 2
NORM_EPS = 1e-6
LN_EPS = 1e-5

kernel_name = 'hybrid_diffattn_pool_conformer_peer_step'


def rmsnorm(x, g):
    xf = x.astype(jnp.float32)
    y = xf * lax.rsqrt(jnp.mean(xf * xf, axis=-1, keepdims=True) + NORM_EPS)
    return (y * g.astype(jnp.float32)).astype(x.dtype)


def layernorm(x, g, b):
    xf = x.astype(jnp.float32)
    mu = jnp.mean(xf, axis=-1, keepdims=True)
    xc = xf - mu
    y = xc * lax.rsqrt(jnp.mean(xc * xc, axis=-1, keepdims=True) + LN_EPS)
    return (y * g.astype(jnp.float32) + b.astype(jnp.float32)).astype(x.dtype)


def modulate(h, shift, scale):
    return h * (1 + scale[:, None, :]) + shift[:, None, :]


def axial_rope_tables(n_tokens):
    rows = n_tokens // GRID_W
    r = jnp.repeat(jnp.arange(rows, dtype=jnp.float32), GRID_W)
    col = jnp.tile(jnp.arange(GRID_W, dtype=jnp.float32), rows)
    freq = ROPE_THETA ** (-jnp.arange(ROPE_AXIS_F, dtype=jnp.float32) / ROPE_AXIS_F)
    ang = jnp.stack([r[:, None] * freq, col[:, None] * freq], axis=1)
    return jnp.cos(ang)[:, None, None, :, None, :], jnp.sin(ang)[:, None, None, :, None, :]


def apply_axial_rope(x, cos, sin):
    B, T, H, C, d = x.shape
    x5 = x.reshape(B, T, H, C, 2, 2, ROPE_AXIS_F)
    rot = jnp.concatenate([-x5[..., 1:, :], x5[..., :1, :]], axis=-2)
    out = x5 * cos.astype(x.dtype) + rot * sin.astype(x.dtype)
    return out.reshape(B, T, H, C, d)


def diff_attention(q, k, v, lam):
    B, T, H, _, d = q.shape
    nb = T // Q_BLOCK
    qb = q.reshape(B, nb, Q_BLOCK, H, 2, d).transpose(1, 0, 2, 3, 4, 5)
    scale = d ** -0.5

    def block(qi):
        s = jnp.einsum('bqhcd,bkhcd->bhcqk', qi, k).astype(jnp.float32) * scale
        pr = jax.nn.softmax(s, axis=-1)
        w = (pr[:, :, 0] - lam * pr[:, :, 1]).astype(v.dtype)
        return jnp.einsum('bhqk,bkhe->bqhe', w, v)

    out = lax.map(block, qb)
    return out.transpose(1, 0, 2, 3, 4).reshape(B, T, H, 2 * d)


def pool_mixer(p, pool_w, pool_scale):
    B, T, C = p.shape
    pg = p.reshape(B, T, N_POOL, POOL_C)
    csum = jnp.concatenate([jnp.zeros((B, 1, N_POOL, POOL_C), jnp.float32),
                            jnp.cumsum(pg.astype(jnp.float32), axis=1)], axis=1)
    t = jnp.arange(T)
    win = jnp.array(POOL_WINDOWS, dtype=jnp.int32)[:, None]
    lo = jnp.clip(t[None, :] - win // 2, 0, T - 1)
    hi = jnp.clip(t[None, :] + win - win // 2 - 1, 0, T - 1)
    gidx = jnp.arange(N_POOL)[:, None]
    total = csum[:, hi + 1, gidx] - csum[:, lo, gidx]
    mean = total / (hi - lo + 1).astype(jnp.float32)[None, :, :, None]
    diff = (mean.transpose(0, 2, 1, 3) - pg.astype(jnp.float32)).astype(p.dtype)
    out = jnp.einsum('btgc,gce->btge', diff, pool_w).reshape(B, T, C)
    return out * pool_scale


def even_mixer(h, w_in, w_out, lam, lam_init, subln_g, pool_w, pool_scale, rope, ctx_k, ctx_v):
    B, T, _ = h.shape
    proj = h @ w_in
    q, k, v, p = jnp.split(proj, [A_QK, 2 * A_QK, 2 * A_QK + A_V], axis=-1)
    q = q.reshape(B, T, A_HEADS, 2, A_HD)
    k = k.reshape(B, T, A_HEADS, 2, A_HD)
    v = v.reshape(B, T, A_HEADS, 2 * A_HD)
    if rope is not None:
        q = apply_axial_rope(q, rope[0], rope[1])
        k = apply_axial_rope(k, rope[0], rope[1])
    if ctx_k is None:
        k_all, v_all = k, v
    else:
        k_all = jnp.concatenate([ctx_k.astype(k.dtype), k], axis=1)
        v_all = jnp.concatenate([ctx_v.astype(v.dtype), v], axis=1)
    a = diff_attention(q, k_all, v_all, lam)
    a = rmsnorm(a, subln_g) * (1.0 - lam_init)
    o_pool = pool_mixer(p, pool_w, pool_scale)
    out = jnp.concatenate([a.reshape(B, T, A_V), o_pool], axis=-1) @ w_out
    return out, k, v


def conformer_conv(h, pw1_w, pw1_b, dw_w, dw_b, ln_g, ln_b, pw2_w, pw2_b):
    u = h @ pw1_w + pw1_b
    a, g = jnp.split(u, 2, axis=-1)
    u = a * jax.nn.sigmoid(g)
    u = lax.conv_general_dilated(u, dw_w[:, None, :], window_strides=(1,),
                                 padding=[(CONV_K // 2, CONV_K // 2)],
                                 dimension_numbers=('NWC', 'WIO', 'NWC'),
                                 feature_group_count=CONV_C) + dw_b
    u = jax.nn.silu(layernorm(u, ln_g, ln_b))
    return u @ pw2_w + pw2_b


def peer(h, wq, keys, u_tab, v_tab):
    B, T, D = h.shape
    n = B * T
    x = h.reshape(n, D)
    q = (x @ wq).reshape(n, PEER_HEADS, 2, PEER_DK // 2).astype(jnp.float32)
    s = jnp.einsum('nhpc,pkc->nhpk', q, keys.astype(jnp.float32))
    sv, si = lax.top_k(s, PEER_TOPK)
    cand_s = (sv[:, :, 0, :, None] + sv[:, :, 1, None, :]).reshape(n, PEER_HEADS, PEER_TOPK * PEER_TOPK)
    cand_i = (si[:, :, 0, :, None] * PEER_NK + si[:, :, 1, None, :]).reshape(n, PEER_HEADS, PEER_TOPK * PEER_TOPK)
    top_s, pos = lax.top_k(cand_s, PEER_TOPK)
    idx = jnp.take_along_axis(cand_i, pos, axis=-1)
    gate = jax.nn.softmax(top_s, axis=-1).astype(h.dtype)
    nb = n // TOKEN_BLOCK

    def block(args):
        xb, ib, gb = args
        act = jnp.einsum('td,thkd->thk', xb, u_tab[ib])
        return jnp.einsum('thk,thkd->td', gb * jax.nn.gelu(act, approximate=False), v_tab[ib])

    y = lax.map(block, (x.reshape(nb, TOKEN_BLOCK, D),
                        idx.reshape(nb, TOKEN_BLOCK, PEER_HEADS, PEER_TOPK),
                        gate.reshape(nb, TOKEN_BLOCK, PEER_HEADS, PEER_TOPK)))
    return y.reshape(B, T, D)


def setup_inputs(seed: int = 0) -> dict:
    key = jax.random.key(seed)
    ks = iter(jax.random.split(key, 40))
    nrm = lambda shape, s: jax.random.normal(next(ks), shape, jnp.float32) * s
    D = D_MODEL
    return {
        'x_prompt': nrm((BATCH, SEQ, D), 1.0),
        'x_sample': nrm((DEC_BATCH, DEC_SEQ, D), 1.0),
        'cache_k': nrm((DEC_BATCH, N_EVEN, PAST_LEN, A_HEADS, 2, A_HD), 1.0),
        'cache_v': nrm((DEC_BATCH, N_EVEN, PAST_LEN, A_HEADS, 2 * A_HD), 1.0),
        'c': nrm((DEC_BATCH, D), 1.0),
        'c_ctx': nrm((D,), 1.0),
        'norm1_g': 1.0 + nrm((DEPTH, D), 0.02),
        'norm2_g': 1.0 + nrm((DEPTH, D), 0.02),
        'final_g': 1.0 + nrm((D,), 0.02),
        'ada_w': nrm((DEPTH, D, 6 * D), D ** -0.5),
        'ada_b': nrm((DEPTH, 6 * D), 0.01),
        'w_in_e': nrm((N_EVEN, D, IN_EVEN), D ** -0.5),
        'w_out_e': nrm((N_EVEN, MIX_EVEN, D), MIX_EVEN ** -0.5),
        'lam_q1': nrm((N_EVEN, A_HD), 0.1),
        'lam_k1': nrm((N_EVEN, A_HD), 0.1),
        'lam_q2': nrm((N_EVEN, A_HD), 0.1),
        'lam_k2': nrm((N_EVEN, A_HD), 0.1),
        'subln_g': 1.0 + nrm((N_EVEN, 2 * A_HD), 0.02),
        'pool_w': nrm((N_EVEN, N_POOL, POOL_C, POOL_C), POOL_C ** -0.5),
        'pool_scale': 1.0 + nrm((N_EVEN, POOL_WIDTH), 0.02),
        'pw1_w': nrm((N_ODD, D, 2 * CONV_C), D ** -0.5),
        'pw1_b': nrm((N_ODD, 2 * CONV_C), 0.01),
        'dw_w': nrm((N_ODD, CONV_K, CONV_C), CONV_K ** -0.5),
        'dw_b': nrm((N_ODD, CONV_C), 0.01),
        'cln_g': 1.0 + nrm((N_ODD, CONV_C), 0.02),
        'cln_b': nrm((N_ODD, CONV_C), 0.01),
        'pw2_w': nrm((N_ODD, CONV_C, D), CONV_C ** -0.5),
        'pw2_b': nrm((N_ODD, D), 0.01),
        'peer_wq': nrm((DEPTH, D, PEER_HEADS * PEER_DK), D ** -0.5),
        'peer_keys': nrm((DEPTH, 2, PEER_NK, PEER_DK // 2), (PEER_DK // 2) ** -0.5),
        'peer_u': nrm((DEPTH, PEER_N, D), D ** -0.5),
        'peer_v': nrm((DEPTH, PEER_N, D), PEER_HEADS ** -0.5),
    }


def reference(x_prompt, x_sample, cache_k, cache_v, c, c_ctx, norm1_g, norm2_g, final_g,
              ada_w, ada_b, w_in_e, w_out_e, lam_q1, lam_k1, lam_q2, lam_k2, subln_g,
              pool_w, pool_scale, pw1_w, pw1_b, dw_w, dw_b, cln_g, cln_b, pw2_w, pw2_b,
              peer_wq, peer_keys, peer_u, peer_v):
    rope = axial_rope_tables(x_sample.shape[1])
    xp, xs = x_prompt, x_sample
    silu_ctx = jax.nn.silu(c_ctx)[None, :]
    silu_c = jax.nn.silu(c)
    new_k, new_v = [], []
    for i in range(DEPTH):
        j = i // 2
        sh1p, sc1p, g1p, sh2p, sc2p, g2p = jnp.split(silu_ctx @ ada_w[i] + ada_b[i], 6, axis=-1)
        sh1s, sc1s, g1s, sh2s, sc2s, g2s = jnp.split(silu_c @ ada_w[i] + ada_b[i], 6, axis=-1)
        hp = modulate(rmsnorm(xp, norm1_g[i]), sh1p, sc1p)
        hs = modulate(rmsnorm(xs, norm1_g[i]), sh1s, sc1s)
        if i % 2 == 0:
            lam_init = 0.8 - 0.6 * math.exp(-0.3 * i)
            lam = (jnp.exp(jnp.sum(lam_q1[j].astype(jnp.float32) * lam_k1[j].astype(jnp.float32)))
                   - jnp.exp(jnp.sum(lam_q2[j].astype(jnp.float32) * lam_k2[j].astype(jnp.float32)))
                   + lam_init)
            mp, kp, vp = even_mixer(hp, w_in_e[j], w_out_e[j], lam, lam_init, subln_g[j],
                                    pool_w[j], pool_scale[j], None, None, None)
            ms, _, _ = even_mixer(hs, w_in_e[j], w_out_e[j], lam, lam_init, subln_g[j],
                                  pool_w[j], pool_scale[j], rope, cache_k[:, j], cache_v[:, j])
            new_k.append(kp)
            new_v.append(vp)
        else:
            mp = conformer_conv(hp, pw1_w[j], pw1_b[j], dw_w[j], dw_b[j], cln_g[j], cln_b[j], pw2_w[j], pw2_b[j])
            ms = conformer_conv(hs, pw1_w[j], pw1_b[j], dw_w[j], dw_b[j], cln_g[j], cln_b[j], pw2_w[j], pw2_b[j])
        xp = xp + g1p[:, None, :] * mp
        xs = xs + g1s[:, None, :] * ms
        xp = xp + g2p[:, None, :] * peer(modulate(rmsnorm(xp, norm2_g[i]), sh2p, sc2p),
                                         peer_wq[i], peer_keys[i], peer_u[i], peer_v[i])
        xs = xs + g2s[:, None, :] * peer(modulate(rmsnorm(xs, norm2_g[i]), sh2s, sc2s),
                                         peer_wq[i], peer_keys[i], peer_u[i], peer_v[i])
    y_prompt = rmsnorm(xp, final_g)
    y_sample = rmsnorm(xs, final_g)
    new_cache_k = jnp.stack(new_k, axis=1)
    new_cache_v = jnp.stack(new_v, axis=1)
    return (y_prompt, y_sample, new_cache_k, new_cache_v)
```

```python
import functools
import math

import jax
import jax.numpy as jnp
from jax import lax
from jax.experimental import pallas as pl
from jax.experimental.pallas import tpu as pltpu

F32 = jnp.float32
BF16 = jnp.bfloat16

D_MODEL = 2048
DEPTH = 2
GRID_W = 64
A_HD = 64
A_HEADS = D_MODEL // (4 * A_HD)
A_QK = A_HEADS * 2 * A_HD
A_V = A_HEADS * 2 * A_HD
POOL_WIDTH = D_MODEL // 2
POOL_WINDOWS = (2, 4, 8, 16)
POOL_C = POOL_WIDTH // len(POOL_WINDOWS)
IN_EVEN = 2 * A_QK + A_V + POOL_WIDTH
CONV_K = 31
PEER_HEADS = 8
PEER_NK = 128
PEER_N = PEER_NK * PEER_NK
PEER_TOPK = 16
ROPE_THETA = 10000.0
ROPE_AXIS_F = A_HD // 4
NORM_EPS = 1e-6
LN_EPS = 1e-5

LANES = 128
MOD_ROWS = 16
VMEM_LIMIT = 56 * 1024 * 1024
HALO = 16
LOG2E = 1.4426950408889634
SQRT_HALF = 0.7071067811865476
NEG_INF = float("-inf")


def _params(*sem):
    return pltpu.CompilerParams(dimension_semantics=sem, vmem_limit_bytes=VMEM_LIMIT)


def _dot(a, b):
    return jnp.dot(a, b, preferred_element_type=F32)


def _dot_nt(a, b):
    return lax.dot_general(a, b, (((1,), (1,)), ((), ())), preferred_element_type=F32)


def _ada_body(c_ref, w_ref, b_ref, o_ref):
    c = c_ref[...]
    s = (c * jax.nn.sigmoid(c)).astype(BF16)
    o_ref[...] = _dot(s, w_ref[...].astype(BF16)) + b_ref[...]


def _ada(cvec, ada_w, ada_b):
    depth, d, n = ada_w.shape
    tn = 1024
    return pl.pallas_call(
        _ada_body,
        grid=(depth, n // tn),
        in_specs=[
            pl.BlockSpec((MOD_ROWS, d), lambda l, j: (0, 0)),
            pl.BlockSpec((None, d, tn), lambda l, j: (l, 0, j)),
            pl.BlockSpec((None, 1, tn), lambda l, j: (l, 0, j)),
        ],
        out_specs=pl.BlockSpec((None, MOD_ROWS, tn), lambda l, j: (l, 0, j)),
        out_shape=jax.ShapeDtypeStruct((depth, MOD_ROWS, n), F32),
        compiler_params=_params("parallel", "parallel"),
        name="ada",
    )(cvec, ada_w, ada_b.reshape(depth, 1, n))


def _linear(xs, w, *, mod_row, tm, tn, prologue="none", epilogue="none", bias=None,
            norm_g=None, norm_b=None, shift=None, scale=None, res=None, gate=None,
            emit_xt=False, out_dtype=F32, name="linear"):
    n_tok = xs[0].shape[0]
    k_tot = sum(x.shape[1] for x in xs)
    n_w = w.shape[1]
    n_out = n_w // 2 if epilogue == "glu" else n_w
    nj = n_out // tn
    row_map = lambda i, j: (i, 0)
    vec_map = lambda i, j: (0, 0)
    mod_map = lambda i, j: (mod_row(i), 0, 0)

    args, specs = [], []
    for x in xs:
        args.append(x)
        specs.append(pl.BlockSpec((tm, x.shape[1]), row_map))
    if prologue == "rms_mod":
        args += [norm_g.reshape(1, k_tot), shift, scale]
        specs += [pl.BlockSpec((1, k_tot), vec_map),
                  pl.BlockSpec((None, 1, k_tot), mod_map),
                  pl.BlockSpec((None, 1, k_tot), mod_map)]
    elif prologue == "ln_silu":
        args += [norm_g.reshape(1, k_tot), norm_b.reshape(1, k_tot)]
        specs += [pl.BlockSpec((1, k_tot), vec_map)] * 2
    args.append(w)
    specs.append(pl.BlockSpec((k_tot, tn), lambda i, j: (0, j)))
    if epilogue == "glu":
        args.append(w)
        specs.append(pl.BlockSpec((k_tot, tn), lambda i, j: (0, j + nj)))
    if bias is not None:
        b2 = bias.reshape(1, n_w)
        args.append(b2)
        specs.append(pl.BlockSpec((1, tn), lambda i, j: (0, j)))
        if epilogue == "glu":
            args.append(b2)
            specs.append(pl.BlockSpec((1, tn), lambda i, j: (0, j + nj)))
    if epilogue == "resid":
        args += [res, gate]
        specs += [pl.BlockSpec((tm, tn), lambda i, j: (i, j)),
                  pl.BlockSpec((None, 1, tn), lambda i, j: (mod_row(i), 0, j))]

    out_shape = [jax.ShapeDtypeStruct((n_tok, n_out), out_dtype)]
    out_specs = [pl.BlockSpec((tm, tn), lambda i, j: (i, j))]
    if emit_xt:
        out_shape.append(jax.ShapeDtypeStruct((k_tot, n_tok), BF16))
        out_specs.append(pl.BlockSpec((k_tot, tm), lambda i, j: (0, i)))
    n_x = len(xs)
    has_bias = bias is not None

    def body(*refs):
        it = iter(refs)
        x_refs = [next(it) for _ in range(n_x)]
        if prologue == "rms_mod":
            g_ref, sh_ref, sc_ref = next(it), next(it), next(it)
        elif prologue == "ln_silu":
            g_ref, b_ref = next(it), next(it)
        w_ref = next(it)
        w2_ref = next(it) if epilogue == "glu" else None
        bias_ref = next(it) if has_bias else None
        bias2_ref = next(it) if (has_bias and epilogue == "glu") else None
        if epilogue == "resid":
            res_ref, gate_ref = next(it), next(it)
        o_ref = next(it)
        xt_ref = next(it) if emit_xt else None
        xm_scr = next(it)

        @pl.when(pl.program_id(1) == 0)
        def _():
            if prologue == "none":
                off = 0
                for xr in x_refs:
                    kk = xr.shape[1]
                    xm_scr[:, off:off + kk] = xr[...].astype(BF16)
                    off += kk
                return
            x = x_refs[0][...].astype(F32)
            if prologue == "rms_mod":
                y = x * lax.rsqrt(jnp.mean(x * x, axis=-1, keepdims=True) + NORM_EPS)
                y = y * g_ref[...]
                y = y * (1.0 + sc_ref[...]) + sh_ref[...]
            else:
                mu = jnp.mean(x, axis=-1, keepdims=True)
                xc = x - mu
                y = xc * lax.rsqrt(jnp.mean(xc * xc, axis=-1, keepdims=True) + LN_EPS)
                y = y * g_ref[...] + b_ref[...]
                y = y * jax.nn.sigmoid(y)
            xm_scr[...] = y.astype(BF16)
            if emit_xt:
                xt_ref[...] = y.T.astype(BF16)

        xm = xm_scr[...]
        y = _dot(xm, w_ref[...])
        if has_bias:
            y = y + bias_ref[...]
        if epilogue == "glu":
            gte = _dot(xm, w2_ref[...])
            if has_bias:
                gte = gte + bias2_ref[...]
            y = y * jax.nn.sigmoid(gte)
        elif epilogue == "resid":
            y = res_ref[...] + gate_ref[...] * y
        o_ref[...] = y.astype(o_ref.dtype)

    outs = pl.pallas_call(
        body,
        grid=(n_tok // tm, nj),
        in_specs=specs,
        out_specs=out_specs,
        out_shape=out_shape,
        scratch_shapes=[pltpu.VMEM((tm, k_tot), BF16)],
        compiler_params=_params("parallel", "arbitrary"),
        name=name,
    )(*args)
    return outs if emit_xt else outs[0]


def _rope_tables(n_tokens):
    t = jnp.arange(n_tokens)
    pos = jnp.stack([(t // GRID_W).astype(F32), (t % GRID_W).astype(F32)], axis=1)
    freq = ROPE_THETA ** (-jnp.arange(ROPE_AXIS_F, dtype=F32) / ROPE_AXIS_F)
    lane = jnp.arange(LANES)
    axis = (lane % A_HD) // (2 * ROPE_AXIS_F)
    half = (lane // ROPE_AXIS_F) % 2
    ang = pos[:, axis] * freq[lane % ROPE_AXIS_F][None, :]
    cos, sin = jnp.cos(ang), jnp.sin(ang)
    sin_lo = jnp.where(half[None, :] == 1, sin, 0.0)
    sin_hi = jnp.where(half[None, :] == 0, -sin, 0.0)
    return cos, sin_lo, sin_hi


def _rope(x, cos, sin_lo, sin_hi):
    return (x * cos + pltpu.roll(x, ROPE_AXIS_F, 1) * sin_lo
            + pltpu.roll(x, LANES - ROPE_AXIS_F, 1) * sin_hi)


def _attention(proj, lamv, subln_g, lam_init, *, tq, ctx=None, rope=None):
    bsz, t_len, _ = proj.shape
    nq = t_len // tq
    has_ctx = ctx is not None
    out_scale = 1.0 - lam_init

    def body(*refs):
        it = iter(refs)
        lam_ref, g_ref, q_ref, k_ref, v_ref = (next(it) for _ in range(5))
        if has_ctx:
            ck_ref, cv_ref = next(it), next(it)
            cq_ref, slq_ref, shq_ref, ck_tab, slk_tab, shk_tab = (next(it) for _ in range(6))
        o_ref = next(it)
        if has_ctx:
            kr_scr = next(it)

            @pl.when(pl.program_id(2) == 0)
            def _():
                kr_scr[...] = _rope(k_ref[...], ck_tab[...], slk_tab[...], shk_tab[...]).astype(BF16)

        lv = lam_ref[...]
        lam = (jnp.exp(jnp.sum(lv[0:1] * lv[1:2], axis=-1, keepdims=True))
               - jnp.exp(jnp.sum(lv[2:3] * lv[3:4], axis=-1, keepdims=True)) + lam_init)
        q = q_ref[...]
        if has_ctx:
            q = _rope(q, cq_ref[...], slq_ref[...], shq_ref[...])
            k_own = kr_scr[...]
        else:
            k_own = k_ref[...].astype(BF16)
        q = q * (A_HD ** -0.5)
        lane = lax.broadcasted_iota(jnp.int32, q.shape, 1)
        v_own = v_ref[...].astype(BF16)
        if has_ctx:
            k_ctx = ck_ref[...].astype(BF16)
            v_ctx = cv_ref[...].astype(BF16)

        probs = []
        for comp in range(2):
            in_comp = (lane < A_HD) if comp == 0 else (lane >= A_HD)
            qc = jnp.where(in_comp, q, 0.0).astype(BF16)
            s_own = _dot_nt(qc, k_own)
            m = jnp.max(s_own, axis=-1, keepdims=True)
            if has_ctx:
                s_ctx = _dot_nt(qc, k_ctx)
                m = jnp.maximum(m, jnp.max(s_ctx, axis=-1, keepdims=True))
            e_own = jnp.exp(s_own - m)
            l = jnp.sum(e_own, axis=-1, keepdims=True)
            e_ctx = None
            if has_ctx:
                e_ctx = jnp.exp(s_ctx - m)
                l = l + jnp.sum(e_ctx, axis=-1, keepdims=True)
            probs.append((e_own, e_ctx, 1.0 / l))
        (e0o, e0c, r0), (e1o, e1c, r1) = probs
        r1 = lam * r1
        out = _dot((e0o * r0 - e1o * r1).astype(BF16), v_own)
        if has_ctx:
            out = out + _dot((e0c * r0 - e1c * r1).astype(BF16), v_ctx)
        out = out * lax.rsqrt(jnp.mean(out * out, axis=-1, keepdims=True) + NORM_EPS)
        o_ref[...] = (out * g_ref[...] * out_scale).astype(o_ref.dtype)

    nh = A_HEADS
    args = [lamv, subln_g.reshape(1, 2 * A_HD), proj, proj, proj]
    specs = [
        pl.BlockSpec((8, LANES), lambda b, h, i: (0, 0)),
        pl.BlockSpec((1, 2 * A_HD), lambda b, h, i: (0, 0)),
        pl.BlockSpec((None, tq, LANES), lambda b, h, i: (b, i, h)),
        pl.BlockSpec((None, t_len, LANES), lambda b, h, i: (b, 0, nh + h)),
        pl.BlockSpec((None, t_len, LANES), lambda b, h, i: (b, 0, 2 * nh + h)),
    ]
    scratch = []
    if has_ctx:
        p_len = ctx[0].shape[1]
        args += [ctx[0], ctx[1]]
        specs += [pl.BlockSpec((None, p_len, LANES), lambda b, h, i: (b, 0, h))] * 2
        args += list(rope) + list(rope)
        specs += [pl.BlockSpec((tq, LANES), lambda b, h, i: (i, 0))] * 3
        specs += [pl.BlockSpec((t_len, LANES), lambda b, h, i: (0, 0))] * 3
        scratch = [pltpu.VMEM((t_len, LANES), BF16)]
    return pl.pallas_call(
        body,
        grid=(bsz, nh, nq),
        in_specs=specs,
        out_specs=pl.BlockSpec((None, tq, LANES), lambda b, h, i: (b, i, h)),
        out_shape=jax.ShapeDtypeStruct((bsz, t_len, A_V), BF16),
        scratch_shapes=scratch,
        compiler_params=_params("parallel", "parallel", "arbitrary"),
        name="attention_ctx" if has_ctx else "attention",
    )(*args)


def _pool(proj, pool_w, pool_scale):
    bsz, t_len, n_in = proj.shape
    col_blk = (n_in - POOL_WIDTH) // POOL_WIDTH

    def body(p_ref, w_ref, sc_ref, o_ref, pad_scr):
        zeros = jnp.zeros((HALO, POOL_C), F32)
        pad_scr[0:HALO, :] = zeros
        pad_scr[HALO + t_len:2 * HALO + t_len, :] = zeros
        tok = lax.broadcasted_iota(jnp.int32, (t_len, POOL_C), 0)
        for g, win in enumerate(POOL_WINDOWS):
            cols = slice(g * POOL_C, (g + 1) * POOL_C)
            x = p_ref[:, cols]
            pad_scr[HALO:HALO + t_len, :] = x
            back, fwd = win // 2, win - win // 2 - 1
            total = pad_scr[pl.ds(HALO - back, t_len), :]
            for o in range(-back + 1, fwd + 1):
                total = total + pad_scr[pl.ds(HALO + o, t_len), :]
            lo = jnp.maximum(tok - back, 0)
            hi = jnp.minimum(tok + fwd, t_len - 1)
            cnt = (hi - lo + 1).astype(F32)
            diff = (total / cnt - x).astype(BF16)
            o_ref[:, cols] = (_dot(diff, w_ref[g]) * sc_ref[:, cols]).astype(o_ref.dtype)

    return pl.pallas_call(
        body,
        grid=(bsz,),
        in_specs=[
            pl.BlockSpec((None, t_len, POOL_WIDTH), lambda b: (b, 0, col_blk)),
            pl.BlockSpec(pool_w.shape, lambda b: (0, 0, 0)),
            pl.BlockSpec((1, POOL_WIDTH), lambda b: (0, 0)),
        ],
        out_specs=pl.BlockSpec((None, t_len, POOL_WIDTH), lambda b: (b, 0, 0)),
        out_shape=jax.ShapeDtypeStruct((bsz, t_len, POOL_WIDTH), BF16),
        scratch_shapes=[pltpu.VMEM((t_len + 2 * HALO, POOL_C), F32)],
        compiler_params=_params("parallel"),
        name="pool",
    )(proj, pool_w, pool_scale.reshape(1, POOL_WIDTH))


def _dwconv(u, dw_w, dw_b, *, cb=512, rows=16):
    bsz, t_len, ch = u.shape
    taps = dw_w.shape[0]
    reach = taps // 2
    assert reach <= HALO

    def body(u_ref, w_ref, b_ref, o_ref, pad_scr):
        zeros = jnp.zeros((HALO, cb), F32)
        pad_scr[0:HALO, :] = zeros
        pad_scr[HALO + t_len:2 * HALO + t_len, :] = zeros
        pad_scr[HALO:HALO + t_len, :] = u_ref[...]

        for base in range(0, t_len, rows):
            acc = jnp.zeros((rows, cb), F32)
            for k in range(taps):
                acc = acc + pad_scr[base + (HALO - reach + k):base + (HALO - reach + k) + rows, :] * w_ref[k:k + 1, :]
            o_ref[base:base + rows, :] = acc + b_ref[...]

    w_pad = jnp.zeros((32, ch), F32).at[:taps].set(dw_w)
    return pl.pallas_call(
        body,
        grid=(bsz, ch // cb),
        in_specs=[
            pl.BlockSpec((None, t_len, cb), lambda b, c: (b, 0, c)),
            pl.BlockSpec((32, cb), lambda b, c: (0, c)),
            pl.BlockSpec((1, cb), lambda b, c: (0, c)),
        ],
        out_specs=pl.BlockSpec((None, t_len, cb), lambda b, c: (b, 0, c)),
        out_shape=jax.ShapeDtypeStruct((bsz, t_len, ch), F32),
        scratch_shapes=[pltpu.VMEM((t_len + 2 * HALO, cb), F32)],
        compiler_params=_params("parallel", "parallel"),
        name="dwconv",
    )(u, w_pad, dw_b.reshape(1, ch))


N_RANK = PEER_TOPK + 1


def _top_values(s, count):
    vals = []
    for _ in range(count):
        m = jnp.max(s, axis=0, keepdims=True)
        vals.append(m)
        s = jnp.where(s == m, NEG_INF, s)
    return vals


def _peer_topk(q, keys, *, tt):
    n_tok = q.shape[0]
    pairs = [(a, b) for a in range(N_RANK) for b in range(N_RANK) if (a + 1) * (b + 1) <= N_RANK]

    n_cand = -(-len(pairs) // 8) * 8

    def body(q_ref, keys_ref, ab_ref, thr_ref, cand_scr):
        cand_scr[...] = jnp.full(cand_scr.shape, NEG_INF, F32)

        def head(h, carry):
            raw, top = [], []
            for p in range(2):
                col = pl.multiple_of((2 * h + p) * PEER_NK, PEER_NK)
                s = _dot_nt(keys_ref[p], q_ref[:, pl.ds(col, PEER_NK)])
                raw.append(s)
                top.append(_top_values(s, N_RANK))
            for idx, (a, b) in enumerate(pairs):
                cand_scr[idx:idx + 1, :] = top[0][a] + top[1][b]
            best = _top_values(cand_scr[...], N_RANK)
            z = jnp.ones_like(best[0])
            for i in range(1, PEER_TOPK):
                z = z + jnp.exp(best[i] - best[0])
            log_z = jnp.log(z)
            mid = 0.5 * (best[PEER_TOPK - 1] + best[PEER_TOPK])
            ab_ref[2 * h] = (raw[0] - top[0][0]) * LOG2E
            ab_ref[2 * h + 1] = (raw[1] - top[1][0] - log_z) * LOG2E
            thr_ref[pl.ds(h, 1), :] = (mid - best[0] - log_z) * LOG2E
            return carry

        lax.fori_loop(0, PEER_HEADS, head, 0)

    return pl.pallas_call(
        body,
        grid=(n_tok // tt,),
        in_specs=[
            pl.BlockSpec((tt, q.shape[1]), lambda i: (i, 0)),
            pl.BlockSpec(keys.shape, lambda i: (0, 0, 0)),
        ],
        out_specs=[
            pl.BlockSpec((2 * PEER_HEADS, PEER_NK, tt), lambda i: (0, 0, i)),
            pl.BlockSpec((PEER_HEADS, tt), lambda i: (0, i)),
        ],
        out_shape=[
            jax.ShapeDtypeStruct((2 * PEER_HEADS, PEER_NK, n_tok), F32),
            jax.ShapeDtypeStruct((PEER_HEADS, n_tok), F32),
        ],
        scratch_shapes=[pltpu.VMEM((n_cand, tt), F32)],
        compiler_params=_params("parallel"),
        name="peer_topk",
    )(q, keys)


def _peer_dense(xt, u_tab, vt_tab, ab, thr, res, gate, *, mod_row, tt, et, lane_chunk=256):
    d, n_tok = xt.shape
    n_exp = u_tab.shape[0]
    rows_per_step = et // PEER_NK
    n_steps = n_exp // et

    def body(xt_ref, u_ref, vt_ref, ab_ref, thr_ref, res_ref, gate_ref, o_ref, acc_scr, act_scr, w_scr):
        j = pl.program_id(1)

        @pl.when(j == 0)
        def _():
            acc_scr[...] = jnp.zeros_like(acc_scr)

        act_scr[...] = _dot(u_ref[...], xt_ref[...])
        for r in range(rows_per_step):
            i1 = j * rows_per_step + r
            rs = slice(r * PEER_NK, (r + 1) * PEER_NK)
            for c in range(tt // lane_chunk):
                cs = slice(c * lane_chunk, (c + 1) * lane_chunk)
                g = jnp.zeros((PEER_NK, lane_chunk), F32)
                for h in range(PEER_HEADS):
                    s = ab_ref[2 * h, pl.ds(i1, 1), cs] + ab_ref[2 * h + 1, :, cs]
                    g = g + jnp.where(s >= thr_ref[h:h + 1, cs], jnp.exp2(s), 0.0)
                a = act_scr[rs, cs]
                gelu = 0.5 * a * (1.0 + lax.erf(a * SQRT_HALF))
                w_scr[rs, cs] = (g * gelu).astype(BF16)
        acc_scr[...] += _dot(vt_ref[...], w_scr[...])

        @pl.when(j == n_steps - 1)
        def _():
            o_ref[...] = res_ref[...] + gate_ref[...] * acc_scr[...].T

    return pl.pallas_call(
        body,
        grid=(n_tok // tt, n_steps),
        in_specs=[
            pl.BlockSpec((d, tt), lambda i, j: (0, i)),
            pl.BlockSpec((et, d), lambda i, j: (j, 0)),
            pl.BlockSpec((d, et), lambda i, j: (0, j)),
            pl.BlockSpec((2 * PEER_HEADS, PEER_NK, tt), lambda i, j: (0, 0, i)),
            pl.BlockSpec((PEER_HEADS, tt), lambda i, j: (0, i)),
            pl.BlockSpec((tt, d), lambda i, j: (i, 0)),
            pl.BlockSpec((None, 1, d), lambda i, j: (mod_row(i), 0, 0)),
        ],
        out_specs=pl.BlockSpec((tt, d), lambda i, j: (i, 0)),
        out_shape=jax.ShapeDtypeStruct((n_tok, d), F32),
        scratch_shapes=[
            pltpu.VMEM((d, tt), F32),
            pltpu.VMEM((et, tt), F32),
            pltpu.VMEM((et, tt), BF16),
        ],
        compiler_params=_params("parallel", "arbitrary"),
        name="peer_dense",
    )(xt, u_tab, vt_tab, ab, thr, res, gate)


def _final_norm(x, g, *, tm):
    n_tok, d = x.shape

    def body(x_ref, g_ref, o_ref):
        xv = x_ref[...]
        o_ref[...] = xv * lax.rsqrt(jnp.mean(xv * xv, axis=-1, keepdims=True) + NORM_EPS) * g_ref[...]

    return pl.pallas_call(
        body,
        grid=(n_tok // tm,),
        in_specs=[pl.BlockSpec((tm, d), lambda i: (i, 0)), pl.BlockSpec((1, d), lambda i: (0, 0))],
        out_specs=pl.BlockSpec((tm, d), lambda i: (i, 0)),
        out_shape=jax.ShapeDtypeStruct((n_tok, d), F32),
        compiler_params=_params("parallel"),
        name="final_norm",
    )(x, g.reshape(1, d))


TM = 512
PEER_ET = 512
TOPK_TT = 256


def kernel(x_prompt, x_sample, cache_k, cache_v, c, c_ctx, norm1_g, norm2_g, final_g,
           ada_w, ada_b, w_in_e, w_out_e, lam_q1, lam_k1, lam_q2, lam_k2, subln_g,
           pool_w, pool_scale, pw1_w, pw1_b, dw_w, dw_b, cln_g, cln_b, pw2_w, pw2_b,
           peer_wq, peer_keys, peer_u, peer_v):
    d = D_MODEL
    bp, tp, _ = x_prompt.shape
    bs, ts, _ = x_sample.shape

    cvec = jnp.zeros((MOD_ROWS, d), F32).at[0].set(c_ctx).at[1:1 + bs].set(c)
    mods = _ada(cvec, ada_w, ada_b)
    mods = mods.reshape(DEPTH, MOD_ROWS, 6, 1, d).transpose(0, 2, 1, 3, 4)

    streams = [
        dict(x=x_prompt.reshape(bp * tp, d), b=bp, t=tp, mod_row=lambda i: 0),
        dict(x=x_sample.reshape(bs * ts, d), b=bs, t=ts, mod_row=lambda i: 1 + (i * TM) // ts),
    ]
    rope = _rope_tables(ts)
    new_k, new_v = [], []

    for i in range(DEPTH):
        j = i // 2
        sh1, sc1, g1, sh2, sc2, g2 = (mods[i, m] for m in range(6))
        wq = peer_wq[i].astype(BF16)
        u_tab = peer_u[i].astype(BF16)
        vt_tab = peer_v[i].T.astype(BF16)
        if i % 2 == 0:
            lam_init = 0.8 - 0.6 * math.exp(-0.3 * i)
            w_in = w_in_e[j].astype(BF16)
            w_out = w_out_e[j].astype(BF16)
            pw = pool_w[j].astype(BF16)
            lamv = jnp.zeros((8, LANES), F32)
            lamv = lamv.at[0, :A_HD].set(lam_q1[j]).at[1, :A_HD].set(lam_k1[j])
            lamv = lamv.at[2, :A_HD].set(lam_q2[j]).at[3, :A_HD].set(lam_k2[j])
        else:
            w1 = pw1_w[j].astype(BF16)
            w2 = pw2_w[j].astype(BF16)

        for si, st in enumerate(streams):
            x, mod_row = st["x"], st["mod_row"]
            if i % 2 == 0:
                proj = _linear([x], w_in, mod_row=mod_row, tm=TM, tn=1024, prologue="rms_mod",
                               norm_g=norm1_g[i], shift=sh1, scale=sc1, name="w_in")
                proj3 = proj.reshape(st["b"], st["t"], IN_EVEN)
                if si == 0:
                    new_k.append(proj3[:, :, A_QK:2 * A_QK].reshape(bp, tp, A_HEADS, 2, A_HD))
                    new_v.append(proj3[:, :, 2 * A_QK:2 * A_QK + A_V].reshape(bp, tp, A_HEADS, 2 * A_HD))
                    att = _attention(proj3, lamv, subln_g[j], lam_init, tq=tp)
                else:
                    ctx = (cache_k[:, j].reshape(bs, -1, A_QK), cache_v[:, j].reshape(bs, -1, A_V))
                    att = _attention(proj3, lamv, subln_g[j], lam_init, tq=512, ctx=ctx, rope=rope)
                pooled = _pool(proj3, pw, pool_scale[j])
                x = _linear([att.reshape(-1, A_V), pooled.reshape(-1, POOL_WIDTH)], w_out,
                            mod_row=mod_row, tm=TM, tn=1024, epilogue="resid", res=x, gate=g1,
                            name="w_out")
            else:
                u = _linear([x], w1, mod_row=mod_row, tm=TM, tn=512, prologue="rms_mod",
                            epilogue="glu", bias=pw1_b[j], norm_g=norm1_g[i], shift=sh1, scale=sc1,
                            name="pw1_glu")
                u = _dwconv(u.reshape(st["b"], st["t"], d), dw_w[j], dw_b[j]).reshape(-1, d)
                x = _linear([u], w2, mod_row=mod_row, tm=TM, tn=1024, prologue="ln_silu",
                            epilogue="resid", bias=pw2_b[j], norm_g=cln_g[j], norm_b=cln_b[j],
                            res=x, gate=g1, name="pw2")

            q, xt = _linear([x], wq, mod_row=mod_row, tm=TM, tn=1024, prologue="rms_mod",
                            norm_g=norm2_g[i], shift=sh2, scale=sc2, emit_xt=True, name="peer_q")
            ab, thr = _peer_topk(q, peer_keys[i], tt=TOPK_TT)
            x = _peer_dense(xt, u_tab, vt_tab, ab, thr, x, g2, mod_row=mod_row, tt=TM, et=PEER_ET)
            st["x"] = x

    y_prompt = _final_norm(streams[0]["x"], final_g, tm=TM).reshape(bp, tp, d)
    y_sample = _final_norm(streams[1]["x"], final_g, tm=TM).reshape(bs, ts, d)
    return (y_prompt, y_sample, jnp.stack(new_k, axis=1), jnp.stack(new_v, axis=1))
```

```python
import functools
import math

import jax
import jax.numpy as jnp
from jax import lax
from jax.experimental import pallas as pl
from jax.experimental.pallas import tpu as pltpu

F32 = jnp.float32
BF16 = jnp.bfloat16

D_MODEL = 2048
DEPTH = 2
GRID_W = 64
A_HD = 64
A_HEADS = D_MODEL // (4 * A_HD)
A_QK = A_HEADS * 2 * A_HD
A_V = A_HEADS * 2 * A_HD
POOL_WIDTH = D_MODEL // 2
POOL_WINDOWS = (2, 4, 8, 16)
POOL_C = POOL_WIDTH // len(POOL_WINDOWS)
IN_EVEN = 2 * A_QK + A_V + POOL_WIDTH
CONV_K = 31
PEER_HEADS = 8
PEER_NK = 128
PEER_N = PEER_NK * PEER_NK
PEER_TOPK = 16
ROPE_THETA = 10000.0
ROPE_AXIS_F = A_HD // 4
NORM_EPS = 1e-6
LN_EPS = 1e-5

LANES = 128
MOD_ROWS = 16
VMEM_LIMIT = 56 * 1024 * 1024
HALO = 16
LOG2E = 1.4426950408889634
SQRT_HALF = 0.7071067811865476
NEG_INF = float("-inf")


def _params(*sem):
    return pltpu.CompilerParams(dimension_semantics=sem, vmem_limit_bytes=VMEM_LIMIT)


def _dot(a, b):
    return jnp.dot(a, b, preferred_element_type=F32)


def _dot_nt(a, b):
    return lax.dot_general(a, b, (((1,), (1,)), ((), ())), preferred_element_type=F32)


def _ada_body(c_ref, w_ref, b_ref, o_ref):
    c = c_ref[...]
    s = (c * jax.nn.sigmoid(c)).astype(BF16)
    o_ref[...] = _dot(s, w_ref[...].astype(BF16)) + b_ref[...]


def _ada(cvec, ada_w, ada_b):
    depth, d, n = ada_w.shape
    tn = 1024
    return pl.pallas_call(
        _ada_body,
        grid=(depth, n // tn),
        in_specs=[
            pl.BlockSpec((MOD_ROWS, d), lambda l, j: (0, 0)),
            pl.BlockSpec((None, d, tn), lambda l, j: (l, 0, j)),
            pl.BlockSpec((None, 1, tn), lambda l, j: (l, 0, j)),
        ],
        out_specs=pl.BlockSpec((None, MOD_ROWS, tn), lambda l, j: (l, 0, j)),
        out_shape=jax.ShapeDtypeStruct((depth, MOD_ROWS, n), F32),
        compiler_params=_params("parallel", "parallel"),
        name="ada",
    )(cvec, ada_w, ada_b.reshape(depth, 1, n))


def _linear(xs, w, *, mod_row, tm, tn, prologue="none", epilogue="none", bias=None,
            norm_g=None, norm_b=None, shift=None, scale=None, res=None, gate=None,
            emit_xt=False, out_dtype=F32, name="linear"):
    n_tok = xs[0].shape[0]
    k_tot = sum(x.shape[1] for x in xs)
    n_w = w.shape[1]
    n_out = n_w // 2 if epilogue == "glu" else n_w
    nj = n_out // tn
    row_map = lambda i, j: (i, 0)
    vec_map = lambda i, j: (0, 0)
    mod_map = lambda i, j: (mod_row(i), 0, 0)

    args, specs = [], []
    for x in xs:
        args.append(x)
        specs.append(pl.BlockSpec((tm, x.shape[1]), row_map))
    if prologue == "rms_mod":
        args += [norm_g.reshape(1, k_tot), shift, scale]
        specs += [pl.BlockSpec((1, k_tot), vec_map),
                  pl.BlockSpec((None, 1, k_tot), mod_map),
                  pl.BlockSpec((None, 1, k_tot), mod_map)]
    elif prologue == "ln_silu":
        args += [norm_g.reshape(1, k_tot), norm_b.reshape(1, k_tot)]
        specs += [pl.BlockSpec((1, k_tot), vec_map)] * 2
    args.append(w)
    specs.append(pl.BlockSpec((k_tot, tn), lambda i, j: (0, j)))
    if epilogue == "glu":
        args.append(w)
        specs.append(pl.BlockSpec((k_tot, tn), lambda i, j: (0, j + nj)))
    if bias is not None:
        b2 = bias.reshape(1, n_w)
        args.append(b2)
        specs.append(pl.BlockSpec((1, tn), lambda i, j: (0, j)))
        if epilogue == "glu":
            args.append(b2)
            specs.append(pl.BlockSpec((1, tn), lambda i, j: (0, j + nj)))
    if epilogue == "resid":
        args += [res, gate]
        specs += [pl.BlockSpec((tm, tn), lambda i, j: (i, j)),
                  pl.BlockSpec((None, 1, tn), lambda i, j: (mod_row(i), 0, j))]

    out_shape = [jax.ShapeDtypeStruct((n_tok, n_out), out_dtype)]
    out_specs = [pl.BlockSpec((tm, tn), lambda i, j: (i, j))]
    if emit_xt:
        out_shape.append(jax.ShapeDtypeStruct((k_tot, n_tok), BF16))
        out_specs.append(pl.BlockSpec((k_tot, tm), lambda i, j: (0, i)))
    n_x = len(xs)
    has_bias = bias is not None

    def body(*refs):
        it = iter(refs)
        x_refs = [next(it) for _ in range(n_x)]
        if prologue == "rms_mod":
            g_ref, sh_ref, sc_ref = next(it), next(it), next(it)
        elif prologue == "ln_silu":
            g_ref, b_ref = next(it), next(it)
        w_ref = next(it)
        w2_ref = next(it) if epilogue == "glu" else None
        bias_ref = next(it) if has_bias else None
        bias2_ref = next(it) if (has_bias and epilogue == "glu") else None
        if epilogue == "resid":
            res_ref, gate_ref = next(it), next(it)
        o_ref = next(it)
        xt_ref = next(it) if emit_xt else None
        xm_scr = next(it)

        @pl.when(pl.program_id(1) == 0)
        def _():
            if prologue == "none":
                off = 0
                for xr in x_refs:
                    kk = xr.shape[1]
                    xm_scr[:, off:off + kk] = xr[...].astype(BF16)
                    off += kk
                return
            x = x_refs[0][...].astype(F32)
            if prologue == "rms_mod":
                y = x * lax.rsqrt(jnp.mean(x * x, axis=-1, keepdims=True) + NORM_EPS)
                y = y * g_ref[...]
                y = y * (1.0 + sc_ref[...]) + sh_ref[...]
            else:
                mu = jnp.mean(x, axis=-1, keepdims=True)
                xc = x - mu
                y = xc * lax.rsqrt(jnp.mean(xc * xc, axis=-1, keepdims=True) + LN_EPS)
                y = y * g_ref[...] + b_ref[...]
                y = y * jax.nn.sigmoid(y)
            xm_scr[...] = y.astype(BF16)
            if emit_xt:
                xt_ref[...] = y.T.astype(BF16)

        xm = xm_scr[...]
        y = _dot(xm, w_ref[...])
        if has_bias:
            y = y + bias_ref[...]
        if epilogue == "glu":
            gte = _dot(xm, w2_ref[...])
            if has_bias:
                gte = gte + bias2_ref[...]
            y = y * jax.nn.sigmoid(gte)
        elif epilogue == "resid":
            y = res_ref[...] + gate_ref[...] * y
        o_ref[...] = y.astype(o_ref.dtype)

    outs = pl.pallas_call(
        body,
        grid=(n_tok // tm, nj),
        in_specs=specs,
        out_specs=out_specs,
        out_shape=out_shape,
        scratch_shapes=[pltpu.VMEM((tm, k_tot), BF16)],
        compiler_params=_params("parallel", "arbitrary"),
        name=name,
    )(*args)
    return outs if emit_xt else outs[0]


def _rope_tables(n_tokens):
    t = jnp.arange(n_tokens)
    pos = jnp.stack([(t // GRID_W).astype(F32), (t % GRID_W).astype(F32)], axis=1)
    freq = ROPE_THETA ** (-jnp.arange(ROPE_AXIS_F, dtype=F32) / ROPE_AXIS_F)
    lane = jnp.arange(LANES)
    axis = (lane % A_HD) // (2 * ROPE_AXIS_F)
    half = (lane // ROPE_AXIS_F) % 2
    ang = pos[:, axis] * freq[lane % ROPE_AXIS_F][None, :]
    cos, sin = jnp.cos(ang), jnp.sin(ang)
    sin_lo = jnp.where(half[None, :] == 1, sin, 0.0)
    sin_hi = jnp.where(half[None, :] == 0, -sin, 0.0)
    return cos, sin_lo, sin_hi


def _rope(x, cos, sin_lo, sin_hi):
    return (x * cos + pltpu.roll(x, ROPE_AXIS_F, 1) * sin_lo
            + pltpu.roll(x, LANES - ROPE_AXIS_F, 1) * sin_hi)


def _attention(proj, lamv, subln_g, lam_init, *, tq, ctx=None, rope=None):
    bsz, t_len, _ = proj.shape
    nq = t_len // tq
    has_ctx = ctx is not None
    out_scale = 1.0 - lam_init

    def body(*refs):
        it = iter(refs)
        lam_ref, g_ref, q_ref, k_ref, v_ref = (next(it) for _ in range(5))
        if has_ctx:
            ck_ref, cv_ref = next(it), next(it)
            cq_ref, slq_ref, shq_ref, ck_tab, slk_tab, shk_tab = (next(it) for _ in range(6))
        o_ref = next(it)
        if has_ctx:
            kr_scr = next(it)

            @pl.when(pl.program_id(2) == 0)
            def _():
                kr_scr[...] = _rope(k_ref[...], ck_tab[...], slk_tab[...], shk_tab[...]).astype(BF16)

        lv = lam_ref[...]
        lam = (jnp.exp(jnp.sum(lv[0:1] * lv[1:2], axis=-1, keepdims=True))
               - jnp.exp(jnp.sum(lv[2:3] * lv[3:4], axis=-1, keepdims=True)) + lam_init)
        q = q_ref[...]
        if has_ctx:
            q = _rope(q, cq_ref[...], slq_ref[...], shq_ref[...])
            k_own = kr_scr[...]
        else:
            k_own = k_ref[...].astype(BF16)
        q = q * (A_HD ** -0.5)
        lane = lax.broadcasted_iota(jnp.int32, q.shape, 1)
        v_own = v_ref[...].astype(BF16)
        if has_ctx:
            k_ctx = ck_ref[...].astype(BF16)
            v_ctx = cv_ref[...].astype(BF16)

        probs = []
        for comp in range(2):
            in_comp = (lane < A_HD) if comp == 0 else (lane >= A_HD)
            qc = jnp.where(in_comp, q, 0.0).astype(BF16)
            s_own = _dot_nt(qc, k_own)
            m = jnp.max(s_own, axis=-1, keepdims=True)
            if has_ctx:
                s_ctx = _dot_nt(qc, k_ctx)
                m = jnp.maximum(m, jnp.max(s_ctx, axis=-1, keepdims=True))
            e_own = jnp.exp(s_own - m)
            l = jnp.sum(e_own, axis=-1, keepdims=True)
            e_ctx = None
            if has_ctx:
                e_ctx = jnp.exp(s_ctx - m)
                l = l + jnp.sum(e_ctx, axis=-1, keepdims=True)
            probs.append((e_own, e_ctx, 1.0 / l))
        (e0o, e0c, r0), (e1o, e1c, r1) = probs
        r1 = lam * r1
        out = _dot((e0o * r0 - e1o * r1).astype(BF16), v_own)
        if has_ctx:
            out = out + _dot((e0c * r0 - e1c * r1).astype(BF16), v_ctx)
        out = out * lax.rsqrt(jnp.mean(out * out, axis=-1, keepdims=True) + NORM_EPS)
        o_ref[...] = (out * g_ref[...] * out_scale).astype(o_ref.dtype)

    nh = A_HEADS
    args = [lamv, subln_g.reshape(1, 2 * A_HD), proj, proj, proj]
    specs = [
        pl.BlockSpec((8, LANES), lambda b, h, i: (0, 0)),
        pl.BlockSpec((1, 2 * A_HD), lambda b, h, i: (0, 0)),
        pl.BlockSpec((None, tq, LANES), lambda b, h, i: (b, i, h)),
        pl.BlockSpec((None, t_len, LANES), lambda b, h, i: (b, 0, nh + h)),
        pl.BlockSpec((None, t_len, LANES), lambda b, h, i: (b, 0, 2 * nh + h)),
    ]
    scratch = []
    if has_ctx:
        p_len = ctx[0].shape[1]
        args += [ctx[0], ctx[1]]
        specs += [pl.BlockSpec((None, p_len, LANES), lambda b, h, i: (b, 0, h))] * 2
        args += list(rope) + list(rope)
        specs += [pl.BlockSpec((tq, LANES), lambda b, h, i: (i, 0))] * 3
        specs += [pl.BlockSpec((t_len, LANES), lambda b, h, i: (0, 0))] * 3
        scratch = [pltpu.VMEM((t_len, LANES), BF16)]
    return pl.pallas_call(
        body,
        grid=(bsz, nh, nq),
        in_specs=specs,
        out_specs=pl.BlockSpec((None, tq, LANES), lambda b, h, i: (b, i, h)),
        out_shape=jax.ShapeDtypeStruct((bsz, t_len, A_V), BF16),
        scratch_shapes=scratch,
        compiler_params=_params("parallel", "parallel", "arbitrary"),
        name="attention_ctx" if has_ctx else "attention",
    )(*args)


def _pool(proj, pool_w, pool_scale):
    bsz, t_len, n_in = proj.shape
    col_blk = (n_in - POOL_WIDTH) // POOL_WIDTH

    def body(p_ref, w_ref, sc_ref, o_ref, pad_scr):
        zeros = jnp.zeros((HALO, POOL_C), F32)
        pad_scr[0:HALO, :] = zeros
        pad_scr[HALO + t_len:2 * HALO + t_len, :] = zeros
        tok = lax.broadcasted_iota(jnp.int32, (t_len, POOL_C), 0)
        for g, win in enumerate(POOL_WINDOWS):
            cols = slice(g * POOL_C, (g + 1) * POOL_C)
            x = p_ref[:, cols]
            pad_scr[HALO:HALO + t_len, :] = x
            back, fwd = win // 2, win - win // 2 - 1
            total = pad_scr[pl.ds(HALO - back, t_len), :]
            for o in range(-back + 1, fwd + 1):
                total = total + pad_scr[pl.ds(HALO + o, t_len), :]
            lo = jnp.maximum(tok - back, 0)
            hi = jnp.minimum(tok + fwd, t_len - 1)
            cnt = (hi - lo + 1).astype(F32)
            diff = (total / cnt - x).astype(BF16)
            o_ref[:, cols] = (_dot(diff, w_ref[g]) * sc_ref[:, cols]).astype(o_ref.dtype)

    return pl.pallas_call(
        body,
        grid=(bsz,),
        in_specs=[
            pl.BlockSpec((None, t_len, POOL_WIDTH), lambda b: (b, 0, col_blk)),
            pl.BlockSpec(pool_w.shape, lambda b: (0, 0, 0)),
            pl.BlockSpec((1, POOL_WIDTH), lambda b: (0, 0)),
        ],
        out_specs=pl.BlockSpec((None, t_len, POOL_WIDTH), lambda b: (b, 0, 0)),
        out_shape=jax.ShapeDtypeStruct((bsz, t_len, POOL_WIDTH), BF16),
        scratch_shapes=[pltpu.VMEM((t_len + 2 * HALO, POOL_C), F32)],
        compiler_params=_params("parallel"),
        name="pool",
    )(proj, pool_w, pool_scale.reshape(1, POOL_WIDTH))


def _dwconv(u, dw_w, dw_b, *, cb=256, rows=64):
    bsz, t_len, ch = u.shape
    taps = dw_w.shape[0]
    reach = taps // 2
    assert reach <= HALO and taps <= 32
    ext = t_len + 2 * HALO - 8

    def body(u_ref, w_ref, b_ref, o_ref, pad_scr, sh_scr):
        zeros = jnp.zeros((HALO, cb), F32)
        pad_scr[0:HALO, :] = zeros
        pad_scr[HALO + t_len:2 * HALO + t_len, :] = zeros
        pad_scr[HALO:HALO + t_len, :] = u_ref[...]
        for b in range(8):
            off = b + HALO - reach
            sh_scr[b] = pad_scr[off:off + ext, :]

        def chunk(r, carry):
            base = pl.multiple_of(r * rows, rows)
            acc = jnp.zeros((rows, cb), F32)
            for k in range(taps):
                a, b = divmod(k, 8)
                acc = acc + sh_scr[b, pl.ds(base + 8 * a, rows), :] * w_ref[k:k + 1, :]
            o_ref[pl.ds(base, rows), :] = acc + b_ref[...]
            return carry

        lax.fori_loop(0, t_len // rows, chunk, 0)

    w_pad = jnp.zeros((32, ch), F32).at[:taps].set(dw_w)
    return pl.pallas_call(
        body,
        grid=(bsz, ch // cb),
        in_specs=[
            pl.BlockSpec((None, t_len, cb), lambda b, c: (b, 0, c)),
            pl.BlockSpec((32, cb), lambda b, c: (0, c)),
            pl.BlockSpec((1, cb), lambda b, c: (0, c)),
        ],
        out_specs=pl.BlockSpec((None, t_len, cb), lambda b, c: (b, 0, c)),
        out_shape=jax.ShapeDtypeStruct((bsz, t_len, ch), F32),
        scratch_shapes=[pltpu.VMEM((t_len + 2 * HALO, cb), F32), pltpu.VMEM((8, ext, cb), F32)],
        compiler_params=_params("parallel", "parallel"),
        name="dwconv",
    )(u, w_pad, dw_b.reshape(1, ch))


N_RANK = PEER_TOPK + 1


def _top_values(s, count):
    vals = []
    for _ in range(count):
        m = jnp.max(s, axis=0, keepdims=True)
        vals.append(m)
        s = jnp.where(s == m, NEG_INF, s)
    return vals


def _peer_topk(q, keys, *, tt):
    n_tok = q.shape[0]
    pairs = [(a, b) for a in range(N_RANK) for b in range(N_RANK) if (a + 1) * (b + 1) <= N_RANK]

    n_cand = -(-len(pairs) // 8) * 8

    def body(q_ref, keys_ref, ab_ref, thr_ref, cand_scr):
        cand_scr[...] = jnp.full(cand_scr.shape, NEG_INF, F32)

        def head(h, carry):
            raw, top = [], []
            for p in range(2):
                col = pl.multiple_of((2 * h + p) * PEER_NK, PEER_NK)
                s = _dot_nt(keys_ref[p], q_ref[:, pl.ds(col, PEER_NK)])
                raw.append(s)
                top.append(_top_values(s, N_RANK))
            for idx, (a, b) in enumerate(pairs):
                cand_scr[idx:idx + 1, :] = top[0][a] + top[1][b]
            best = _top_values(cand_scr[...], N_RANK)
            z = jnp.ones_like(best[0])
            for i in range(1, PEER_TOPK):
                z = z + jnp.exp(best[i] - best[0])
            log_z = jnp.log(z)
            mid = 0.5 * (best[PEER_TOPK - 1] + best[PEER_TOPK])
            ab_ref[2 * h] = (raw[0] - top[0][0]) * LOG2E
            ab_ref[2 * h + 1] = (raw[1] - top[1][0] - log_z) * LOG2E
            thr_ref[pl.ds(h, 1), :] = (mid - best[0] - log_z) * LOG2E
            return carry

        lax.fori_loop(0, PEER_HEADS, head, 0)

    return pl.pallas_call(
        body,
        grid=(n_tok // tt,),
        in_specs=[
            pl.BlockSpec((tt, q.shape[1]), lambda i: (i, 0)),
            pl.BlockSpec(keys.shape, lambda i: (0, 0, 0)),
        ],
        out_specs=[
            pl.BlockSpec((2 * PEER_HEADS, PEER_NK, tt), lambda i: (0, 0, i)),
            pl.BlockSpec((PEER_HEADS, tt), lambda i: (0, i)),
        ],
        out_shape=[
            jax.ShapeDtypeStruct((2 * PEER_HEADS, PEER_NK, n_tok), F32),
            jax.ShapeDtypeStruct((PEER_HEADS, n_tok), F32),
        ],
        scratch_shapes=[pltpu.VMEM((n_cand, tt), F32)],
        compiler_params=_params("parallel"),
        name="peer_topk",
    )(q, keys)


def _peer_dense(xt, u_tab, v_tab, ab, thr, res, gate, *, mod_row, tt, et, lane_chunk=128):
    d, n_tok = xt.shape
    n_exp = u_tab.shape[0]
    rows_per_step = et // PEER_NK
    n_blk = n_exp // et
    n_steps = n_blk + 2
    n_lane = tt // lane_chunk
    c_cols = d // rows_per_step
    assert 8 % rows_per_step == 0 and n_lane % 2 == 0

    def body(xt_ref, u_ref, v_ref, ab_ref, thr_ref, res_ref, gate_ref, o_ref, acc_scr, act0, act1, w0, w1):
        s = pl.program_id(1)
        act_scr, w_scr = (act0, act1), (w0, w1)

        @pl.when(s == 0)
        def _():
            acc_scr[...] = jnp.zeros_like(acc_scr)
            for ref in act_scr + w_scr:
                ref[...] = jnp.zeros_like(ref)

        def step(cur, prev):
            grp = pl.multiple_of(jnp.clip(((s - 1) * rows_per_step) // 8, 0, PEER_NK // 8 - 1) * 8, 8)
            sub = (prev * rows_per_step) % 8

            def stage_b(r, c):
                rs = slice(r * PEER_NK, (r + 1) * PEER_NK)
                cs = slice(c * lane_chunk, (c + 1) * lane_chunk)
                g = jnp.zeros((PEER_NK, lane_chunk), F32)
                for h in range(PEER_HEADS):
                    a_rows = ab_ref[2 * h, pl.ds(grp, 8), cs]
                    sc = a_rows[sub + r:sub + r + 1] + ab_ref[2 * h + 1, :, cs]
                    g = g + jnp.where(sc >= thr_ref[h:h + 1, cs], jnp.exp2(sc), 0.0)
                a = act_scr[prev][rs, cs]
                gelu = 0.5 * a * (1.0 + lax.erf(a * SQRT_HALF))
                w_scr[prev][rs, cs] = (g * gelu).astype(BF16)

            for r in range(rows_per_step):
                rs = slice(r * PEER_NK, (r + 1) * PEER_NK)
                act_scr[cur][rs, :] = _dot(u_ref[rs, :], xt_ref[...])
                for c in range(n_lane // 2):
                    stage_b(r, c)
                ds = slice(r * c_cols, (r + 1) * c_cols)
                acc_scr[:, ds] += lax.dot_general(w_scr[cur][...], v_ref[:, ds], (((0,), (0,)), ((), ())),
                                                  preferred_element_type=F32)
                for c in range(n_lane // 2, n_lane):
                    stage_b(r, c)

        @pl.when(s % 2 == 0)
        def _():
            step(0, 1)

        @pl.when(s % 2 == 1)
        def _():
            step(1, 0)

        @pl.when(s == n_steps - 1)
        def _():
            o_ref[...] = res_ref[...] + gate_ref[...] * acc_scr[...]

    return pl.pallas_call(
        body,
        grid=(n_tok // tt, n_steps),
        in_specs=[
            pl.BlockSpec((d, tt), lambda i, s: (0, i)),
            pl.BlockSpec((et, d), lambda i, s: (jnp.minimum(s, n_blk - 1), 0)),
            pl.BlockSpec((et, d), lambda i, s: (jnp.maximum(s - 2, 0), 0)),
            pl.BlockSpec((2 * PEER_HEADS, PEER_NK, tt), lambda i, s: (0, 0, i)),
            pl.BlockSpec((PEER_HEADS, tt), lambda i, s: (0, i)),
            pl.BlockSpec((tt, d), lambda i, s: (i, 0)),
            pl.BlockSpec((None, 1, d), lambda i, s: (mod_row(i), 0, 0)),
        ],
        out_specs=pl.BlockSpec((tt, d), lambda i, s: (i, 0)),
        out_shape=jax.ShapeDtypeStruct((n_tok, d), F32),
        scratch_shapes=[
            pltpu.VMEM((tt, d), F32),
            pltpu.VMEM((et, tt), F32),
            pltpu.VMEM((et, tt), F32),
            pltpu.VMEM((et, tt), BF16),
            pltpu.VMEM((et, tt), BF16),
        ],
        compiler_params=_params("parallel", "arbitrary"),
        name="peer_dense",
    )(xt, u_tab, v_tab, ab, thr, res, gate)


def _final_norm(x, g, *, tm):
    n_tok, d = x.shape

    def body(x_ref, g_ref, o_ref):
        xv = x_ref[...]
        o_ref[...] = xv * lax.rsqrt(jnp.mean(xv * xv, axis=-1, keepdims=True) + NORM_EPS) * g_ref[...]

    return pl.pallas_call(
        body,
        grid=(n_tok // tm,),
        in_specs=[pl.BlockSpec((tm, d), lambda i: (i, 0)), pl.BlockSpec((1, d), lambda i: (0, 0))],
        out_specs=pl.BlockSpec((tm, d), lambda i: (i, 0)),
        out_shape=jax.ShapeDtypeStruct((n_tok, d), F32),
        compiler_params=_params("parallel"),
        name="final_norm",
    )(x, g.reshape(1, d))


TM = 512
PEER_ET = 512
TOPK_TT = 256


def kernel(x_prompt, x_sample, cache_k, cache_v, c, c_ctx, norm1_g, norm2_g, final_g,
           ada_w, ada_b, w_in_e, w_out_e, lam_q1, lam_k1, lam_q2, lam_k2, subln_g,
           pool_w, pool_scale, pw1_w, pw1_b, dw_w, dw_b, cln_g, cln_b, pw2_w, pw2_b,
           peer_wq, peer_keys, peer_u, peer_v):
    d = D_MODEL
    bp, tp, _ = x_prompt.shape
    bs, ts, _ = x_sample.shape

    cvec = jnp.zeros((MOD_ROWS, d), F32).at[0].set(c_ctx).at[1:1 + bs].set(c)
    mods = _ada(cvec, ada_w, ada_b)
    mods = mods.reshape(DEPTH, MOD_ROWS, 6, 1, d).transpose(0, 2, 1, 3, 4)

    streams = [
        dict(x=x_prompt.reshape(bp * tp, d), b=bp, t=tp, mod_row=lambda i: 0),
        dict(x=x_sample.reshape(bs * ts, d), b=bs, t=ts, mod_row=lambda i: 1 + (i * TM) // ts),
    ]
    rope = _rope_tables(ts)
    new_k, new_v = [], []

    for i in range(DEPTH):
        j = i // 2
        sh1, sc1, g1, sh2, sc2, g2 = (mods[i, m] for m in range(6))
        wq = peer_wq[i].astype(BF16)
        u_tab = peer_u[i].astype(BF16)
        v_tab = peer_v[i].astype(BF16)
        if i % 2 == 0:
            lam_init = 0.8 - 0.6 * math.exp(-0.3 * i)
            w_in = w_in_e[j].astype(BF16)
            w_out = w_out_e[j].astype(BF16)
            pw = pool_w[j].astype(BF16)
            lamv = jnp.zeros((8, LANES), F32)
            lamv = lamv.at[0, :A_HD].set(lam_q1[j]).at[1, :A_HD].set(lam_k1[j])
            lamv = lamv.at[2, :A_HD].set(lam_q2[j]).at[3, :A_HD].set(lam_k2[j])
        else:
            w1 = pw1_w[j].astype(BF16)
            w2 = pw2_w[j].astype(BF16)

        for si, st in enumerate(streams):
            x, mod_row = st["x"], st["mod_row"]
            if i % 2 == 0:
                proj = _linear([x], w_in, mod_row=mod_row, tm=TM, tn=1024, prologue="rms_mod",
                               norm_g=norm1_g[i], shift=sh1, scale=sc1, name="w_in")
                proj3 = proj.reshape(st["b"], st["t"], IN_EVEN)
                if si == 0:
                    new_k.append(proj3[:, :, A_QK:2 * A_QK].reshape(bp, tp, A_HEADS, 2, A_HD))
                    new_v.append(proj3[:, :, 2 * A_QK:2 * A_QK + A_V].reshape(bp, tp, A_HEADS, 2 * A_HD))
                    att = _attention(proj3, lamv, subln_g[j], lam_init, tq=tp)
                else:
                    ctx = (cache_k[:, j].reshape(bs, -1, A_QK), cache_v[:, j].reshape(bs, -1, A_V))
                    att = _attention(proj3, lamv, subln_g[j], lam_init, tq=512, ctx=ctx, rope=rope)
                pooled = _pool(proj3, pw, pool_scale[j])
                x = _linear([att.reshape(-1, A_V), pooled.reshape(-1, POOL_WIDTH)], w_out,
                            mod_row=mod_row, tm=TM, tn=1024, epilogue="resid", res=x, gate=g1,
                            name="w_out")
            else:
                u = _linear([x], w1, mod_row=mod_row, tm=TM, tn=512, prologue="rms_mod",
                            epilogue="glu", bias=pw1_b[j], norm_g=norm1_g[i], shift=sh1, scale=sc1,
                            name="pw1_glu")
                u = _dwconv(u.reshape(st["b"], st["t"], d), dw_w[j], dw_b[j]).reshape(-1, d)
                x = _linear([u], w2, mod_row=mod_row, tm=TM, tn=1024, prologue="ln_silu",
                            epilogue="resid", bias=pw2_b[j], norm_g=cln_g[j], norm_b=cln_b[j],
                            res=x, gate=g1, name="pw2")

            q, xt = _linear([x], wq, mod_row=mod_row, tm=TM, tn=1024, prologue="rms_mod",
                            norm_g=norm2_g[i], shift=sh2, scale=sc2, emit_xt=True, name="peer_q")
            ab, thr = _peer_topk(q, peer_keys[i], tt=TOPK_TT)
            x = _peer_dense(xt, u_tab, v_tab, ab, thr, x, g2, mod_row=mod_row, tt=TM, et=PEER_ET)
            st["x"] = x

    y_prompt = _final_norm(streams[0]["x"], final_g, tm=TM).reshape(bp, tp, d)
    y_sample = _final_norm(streams[1]["x"], final_g, tm=TM).reshape(bs, ts, d)
    return (y_prompt, y_sample, jnp.stack(new_k, axis=1), jnp.stack(new_v, axis=1))
```

```python
import functools
import math

import jax
import jax.numpy as jnp
from jax import lax
from jax.experimental import pallas as pl
from jax.experimental.pallas import tpu as pltpu

F32 = jnp.float32
BF16 = jnp.bfloat16

D_MODEL = 2048
DEPTH = 2
GRID_W = 64
A_HD = 64
A_HEADS = D_MODEL // (4 * A_HD)
A_QK = A_HEADS * 2 * A_HD
A_V = A_HEADS * 2 * A_HD
POOL_WIDTH = D_MODEL // 2
POOL_WINDOWS = (2, 4, 8, 16)
POOL_C = POOL_WIDTH // len(POOL_WINDOWS)
IN_EVEN = 2 * A_QK + A_V + POOL_WIDTH
CONV_K = 31
PEER_HEADS = 8
PEER_NK = 128
PEER_N = PEER_NK * PEER_NK
PEER_TOPK = 16
ROPE_THETA = 10000.0
ROPE_AXIS_F = A_HD // 4
NORM_EPS = 1e-6
LN_EPS = 1e-5

LANES = 128
MOD_ROWS = 16
VMEM_LIMIT = 56 * 1024 * 1024
HALO = 16
LOG2E = 1.4426950408889634
SQRT_HALF = 0.7071067811865476
NEG_INF = float("-inf")


def _params(*sem):
    return pltpu.CompilerParams(dimension_semantics=sem, vmem_limit_bytes=VMEM_LIMIT)


def _dot(a, b):
    return jnp.dot(a, b, preferred_element_type=F32)


def _dot_nt(a, b):
    return lax.dot_general(a, b, (((1,), (1,)), ((), ())), preferred_element_type=F32)


def _ada_body(c_ref, w_ref, b_ref, o_ref):
    c = c_ref[...]
    s = (c * jax.nn.sigmoid(c)).astype(BF16)
    o_ref[...] = _dot(s, w_ref[...].astype(BF16)) + b_ref[...]


def _ada(cvec, ada_w, ada_b):
    depth, d, n = ada_w.shape
    tn = 1024
    return pl.pallas_call(
        _ada_body,
        grid=(depth, n // tn),
        in_specs=[
            pl.BlockSpec((MOD_ROWS, d), lambda l, j: (0, 0)),
            pl.BlockSpec((None, d, tn), lambda l, j: (l, 0, j)),
            pl.BlockSpec((None, 1, tn), lambda l, j: (l, 0, j)),
        ],
        out_specs=pl.BlockSpec((None, MOD_ROWS, tn), lambda l, j: (l, 0, j)),
        out_shape=jax.ShapeDtypeStruct((depth, MOD_ROWS, n), F32),
        compiler_params=_params("parallel", "parallel"),
        name="ada",
    )(cvec, ada_w, ada_b.reshape(depth, 1, n))


def _linear(xs, w, *, mod_row, tm, tn, prologue="none", epilogue="none", bias=None,
            norm_g=None, norm_b=None, shift=None, scale=None, res=None, gate=None,
            emit_xt=False, out_dtype=F32, name="linear"):
    n_tok = xs[0].shape[0]
    k_tot = sum(x.shape[1] for x in xs)
    n_w = w.shape[1]
    n_out = n_w // 2 if epilogue == "glu" else n_w
    nj = n_out // tn
    row_map = lambda i, j: (i, 0)
    vec_map = lambda i, j: (0, 0)
    mod_map = lambda i, j: (mod_row(i), 0, 0)

    args, specs = [], []
    for x in xs:
        args.append(x)
        specs.append(pl.BlockSpec((tm, x.shape[1]), row_map))
    if prologue == "rms_mod":
        args += [norm_g.reshape(1, k_tot), shift, scale]
        specs += [pl.BlockSpec((1, k_tot), vec_map),
                  pl.BlockSpec((None, 1, k_tot), mod_map),
                  pl.BlockSpec((None, 1, k_tot), mod_map)]
    elif prologue == "ln_silu":
        args += [norm_g.reshape(1, k_tot), norm_b.reshape(1, k_tot)]
        specs += [pl.BlockSpec((1, k_tot), vec_map)] * 2
    args.append(w)
    specs.append(pl.BlockSpec((k_tot, tn), lambda i, j: (0, j)))
    if epilogue == "glu":
        args.append(w)
        specs.append(pl.BlockSpec((k_tot, tn), lambda i, j: (0, j + nj)))
    if bias is not None:
        b2 = bias.reshape(1, n_w)
        args.append(b2)
        specs.append(pl.BlockSpec((1, tn), lambda i, j: (0, j)))
        if epilogue == "glu":
            args.append(b2)
            specs.append(pl.BlockSpec((1, tn), lambda i, j: (0, j + nj)))
    if epilogue == "resid":
        args += [res, gate]
        specs += [pl.BlockSpec((tm, tn), lambda i, j: (i, j)),
                  pl.BlockSpec((None, 1, tn), lambda i, j: (mod_row(i), 0, j))]

    out_shape = [jax.ShapeDtypeStruct((n_tok, n_out), out_dtype)]
    out_specs = [pl.BlockSpec((tm, tn), lambda i, j: (i, j))]
    if emit_xt:
        out_shape.append(jax.ShapeDtypeStruct((k_tot, n_tok), BF16))
        out_specs.append(pl.BlockSpec((k_tot, tm), lambda i, j: (0, i)))
    n_x = len(xs)
    has_bias = bias is not None

    def body(*refs):
        it = iter(refs)
        x_refs = [next(it) for _ in range(n_x)]
        if prologue == "rms_mod":
            g_ref, sh_ref, sc_ref = next(it), next(it), next(it)
        elif prologue == "ln_silu":
            g_ref, b_ref = next(it), next(it)
        w_ref = next(it)
        w2_ref = next(it) if epilogue == "glu" else None
        bias_ref = next(it) if has_bias else None
        bias2_ref = next(it) if (has_bias and epilogue == "glu") else None
        if epilogue == "resid":
            res_ref, gate_ref = next(it), next(it)
        o_ref = next(it)
        xt_ref = next(it) if emit_xt else None
        xm_scr = next(it)

        @pl.when(pl.program_id(1) == 0)
        def _():
            if prologue == "none":
                off = 0
                for xr in x_refs:
                    kk = xr.shape[1]
                    xm_scr[:, off:off + kk] = xr[...].astype(BF16)
                    off += kk
                return
            x = x_refs[0][...].astype(F32)
            if prologue == "rms_mod":
                y = x * lax.rsqrt(jnp.mean(x * x, axis=-1, keepdims=True) + NORM_EPS)
                y = y * g_ref[...]
                y = y * (1.0 + sc_ref[...]) + sh_ref[...]
            else:
                mu = jnp.mean(x, axis=-1, keepdims=True)
                xc = x - mu
                y = xc * lax.rsqrt(jnp.mean(xc * xc, axis=-1, keepdims=True) + LN_EPS)
                y = y * g_ref[...] + b_ref[...]
                y = y * jax.nn.sigmoid(y)
            xm_scr[...] = y.astype(BF16)
            if emit_xt:
                xt_ref[...] = y.T.astype(BF16)

        xm = xm_scr[...]
        y = _dot(xm, w_ref[...])
        if has_bias:
            y = y + bias_ref[...]
        if epilogue == "glu":
            gte = _dot(xm, w2_ref[...])
            if has_bias:
                gte = gte + bias2_ref[...]
            y = y * jax.nn.sigmoid(gte)
        elif epilogue == "resid":
            y = res_ref[...] + gate_ref[...] * y
        o_ref[...] = y.astype(o_ref.dtype)

    outs = pl.pallas_call(
        body,
        grid=(n_tok // tm, nj),
        in_specs=specs,
        out_specs=out_specs,
        out_shape=out_shape,
        scratch_shapes=[pltpu.VMEM((tm, k_tot), BF16)],
        compiler_params=_params("parallel", "arbitrary"),
        name=name,
    )(*args)
    return outs if emit_xt else outs[0]


def _rope_tables(n_tokens):
    t = jnp.arange(n_tokens)
    pos = jnp.stack([(t // GRID_W).astype(F32), (t % GRID_W).astype(F32)], axis=1)
    freq = ROPE_THETA ** (-jnp.arange(ROPE_AXIS_F, dtype=F32) / ROPE_AXIS_F)
    lane = jnp.arange(LANES)
    axis = (lane % A_HD) // (2 * ROPE_AXIS_F)
    half = (lane // ROPE_AXIS_F) % 2
    ang = pos[:, axis] * freq[lane % ROPE_AXIS_F][None, :]
    cos, sin = jnp.cos(ang), jnp.sin(ang)
    sin_lo = jnp.where(half[None, :] == 1, sin, 0.0)
    sin_hi = jnp.where(half[None, :] == 0, -sin, 0.0)
    return cos, sin_lo, sin_hi


def _rope(x, cos, sin_lo, sin_hi):
    return (x * cos + pltpu.roll(x, ROPE_AXIS_F, 1) * sin_lo
            + pltpu.roll(x, LANES - ROPE_AXIS_F, 1) * sin_hi)


def _attention(proj, lamv, subln_g, lam_init, *, tq, ctx=None, rope=None):
    bsz, t_len, _ = proj.shape
    nq = t_len // tq
    has_ctx = ctx is not None
    out_scale = 1.0 - lam_init

    def body(*refs):
        it = iter(refs)
        lam_ref, g_ref, q_ref, k_ref, v_ref = (next(it) for _ in range(5))
        if has_ctx:
            ck_ref, cv_ref = next(it), next(it)
            cq_ref, slq_ref, shq_ref, ck_tab, slk_tab, shk_tab = (next(it) for _ in range(6))
        o_ref = next(it)
        if has_ctx:
            kr_scr = next(it)

            @pl.when(pl.program_id(2) == 0)
            def _():
                kr_scr[...] = _rope(k_ref[...], ck_tab[...], slk_tab[...], shk_tab[...]).astype(BF16)

        lv = lam_ref[...]
        lam = (jnp.exp(jnp.sum(lv[0:1] * lv[1:2], axis=-1, keepdims=True))
               - jnp.exp(jnp.sum(lv[2:3] * lv[3:4], axis=-1, keepdims=True)) + lam_init)
        q = q_ref[...]
        if has_ctx:
            q = _rope(q, cq_ref[...], slq_ref[...], shq_ref[...])
            k_own = kr_scr[...]
        else:
            k_own = k_ref[...].astype(BF16)
        q = q * (A_HD ** -0.5)
        lane = lax.broadcasted_iota(jnp.int32, q.shape, 1)
        v_own = v_ref[...].astype(BF16)
        if has_ctx:
            k_ctx = ck_ref[...].astype(BF16)
            v_ctx = cv_ref[...].astype(BF16)

        probs = []
        for comp in range(2):
            in_comp = (lane < A_HD) if comp == 0 else (lane >= A_HD)
            qc = jnp.where(in_comp, q, 0.0).astype(BF16)
            s_own = _dot_nt(qc, k_own)
            m = jnp.max(s_own, axis=-1, keepdims=True)
            if has_ctx:
                s_ctx = _dot_nt(qc, k_ctx)
                m = jnp.maximum(m, jnp.max(s_ctx, axis=-1, keepdims=True))
            e_own = jnp.exp(s_own - m)
            l = jnp.sum(e_own, axis=-1, keepdims=True)
            e_ctx = None
            if has_ctx:
                e_ctx = jnp.exp(s_ctx - m)
                l = l + jnp.sum(e_ctx, axis=-1, keepdims=True)
            probs.append((e_own, e_ctx, 1.0 / l))
        (e0o, e0c, r0), (e1o, e1c, r1) = probs
        r1 = lam * r1
        out = _dot((e0o * r0 - e1o * r1).astype(BF16), v_own)
        if has_ctx:
            out = out + _dot((e0c * r0 - e1c * r1).astype(BF16), v_ctx)
        out = out * lax.rsqrt(jnp.mean(out * out, axis=-1, keepdims=True) + NORM_EPS)
        o_ref[...] = (out * g_ref[...] * out_scale).astype(o_ref.dtype)

    nh = A_HEADS
    args = [lamv, subln_g.reshape(1, 2 * A_HD), proj, proj, proj]
    specs = [
        pl.BlockSpec((8, LANES), lambda b, h, i: (0, 0)),
        pl.BlockSpec((1, 2 * A_HD), lambda b, h, i: (0, 0)),
        pl.BlockSpec((None, tq, LANES), lambda b, h, i: (b, i, h)),
        pl.BlockSpec((None, t_len, LANES), lambda b, h, i: (b, 0, nh + h)),
        pl.BlockSpec((None, t_len, LANES), lambda b, h, i: (b, 0, 2 * nh + h)),
    ]
    scratch = []
    if has_ctx:
        p_len = ctx[0].shape[1]
        args += [ctx[0], ctx[1]]
        specs += [pl.BlockSpec((None, p_len, LANES), lambda b, h, i: (b, 0, h))] * 2
        args += list(rope) + list(rope)
        specs += [pl.BlockSpec((tq, LANES), lambda b, h, i: (i, 0))] * 3
        specs += [pl.BlockSpec((t_len, LANES), lambda b, h, i: (0, 0))] * 3
        scratch = [pltpu.VMEM((t_len, LANES), BF16)]
    return pl.pallas_call(
        body,
        grid=(bsz, nh, nq),
        in_specs=specs,
        out_specs=pl.BlockSpec((None, tq, LANES), lambda b, h, i: (b, i, h)),
        out_shape=jax.ShapeDtypeStruct((bsz, t_len, A_V), BF16),
        scratch_shapes=scratch,
        compiler_params=_params("parallel", "parallel", "arbitrary"),
        name="attention_ctx" if has_ctx else "attention",
    )(*args)


def _pool(proj, pool_w, pool_scale):
    bsz, t_len, n_in = proj.shape
    col_blk = (n_in - POOL_WIDTH) // POOL_WIDTH

    def body(p_ref, w_ref, sc_ref, o_ref, pad_scr):
        zeros = jnp.zeros((HALO, POOL_C), F32)
        pad_scr[0:HALO, :] = zeros
        pad_scr[HALO + t_len:2 * HALO + t_len, :] = zeros
        tok = lax.broadcasted_iota(jnp.int32, (t_len, POOL_C), 0)
        for g, win in enumerate(POOL_WINDOWS):
            cols = slice(g * POOL_C, (g + 1) * POOL_C)
            x = p_ref[:, cols]
            pad_scr[HALO:HALO + t_len, :] = x
            back, fwd = win // 2, win - win // 2 - 1
            total = pad_scr[pl.ds(HALO - back, t_len), :]
            for o in range(-back + 1, fwd + 1):
                total = total + pad_scr[pl.ds(HALO + o, t_len), :]
            lo = jnp.maximum(tok - back, 0)
            hi = jnp.minimum(tok + fwd, t_len - 1)
            cnt = (hi - lo + 1).astype(F32)
            diff = (total / cnt - x).astype(BF16)
            o_ref[:, cols] = (_dot(diff, w_ref[g]) * sc_ref[:, cols]).astype(o_ref.dtype)

    return pl.pallas_call(
        body,
        grid=(bsz,),
        in_specs=[
            pl.BlockSpec((None, t_len, POOL_WIDTH), lambda b: (b, 0, col_blk)),
            pl.BlockSpec(pool_w.shape, lambda b: (0, 0, 0)),
            pl.BlockSpec((1, POOL_WIDTH), lambda b: (0, 0)),
        ],
        out_specs=pl.BlockSpec((None, t_len, POOL_WIDTH), lambda b: (b, 0, 0)),
        out_shape=jax.ShapeDtypeStruct((bsz, t_len, POOL_WIDTH), BF16),
        scratch_shapes=[pltpu.VMEM((t_len + 2 * HALO, POOL_C), F32)],
        compiler_params=_params("parallel"),
        name="pool",
    )(proj, pool_w, pool_scale.reshape(1, POOL_WIDTH))


def _dwconv(u, dw_w, dw_b, *, cb=256, rows=64):
    bsz, t_len, ch = u.shape
    taps = dw_w.shape[0]
    reach = taps // 2
    assert reach <= HALO and taps <= 32
    ext = t_len + 2 * HALO - 8

    def body(u_ref, w_ref, b_ref, o_ref, pad_scr, sh_scr):
        zeros = jnp.zeros((HALO, cb), F32)
        pad_scr[0:HALO, :] = zeros
        pad_scr[HALO + t_len:2 * HALO + t_len, :] = zeros
        pad_scr[HALO:HALO + t_len, :] = u_ref[...].astype(F32)
        for b in range(8):
            off = b + HALO - reach
            sh_scr[b] = pad_scr[off:off + ext, :]

        def chunk(r, carry):
            base = pl.multiple_of(r * rows, rows)
            acc = jnp.zeros((rows, cb), F32)
            for k in range(taps):
                a, b = divmod(k, 8)
                acc = acc + sh_scr[b, pl.ds(base + 8 * a, rows), :] * w_ref[k:k + 1, :]
            o_ref[pl.ds(base, rows), :] = (acc + b_ref[...]).astype(o_ref.dtype)
            return carry

        lax.fori_loop(0, t_len // rows, chunk, 0)

    w_pad = jnp.zeros((32, ch), F32).at[:taps].set(dw_w)
    return pl.pallas_call(
        body,
        grid=(bsz, ch // cb),
        in_specs=[
            pl.BlockSpec((None, t_len, cb), lambda b, c: (b, 0, c)),
            pl.BlockSpec((32, cb), lambda b, c: (0, c)),
            pl.BlockSpec((1, cb), lambda b, c: (0, c)),
        ],
        out_specs=pl.BlockSpec((None, t_len, cb), lambda b, c: (b, 0, c)),
        out_shape=jax.ShapeDtypeStruct((bsz, t_len, ch), u.dtype),
        scratch_shapes=[pltpu.VMEM((t_len + 2 * HALO, cb), F32), pltpu.VMEM((8, ext, cb), F32)],
        compiler_params=_params("parallel", "parallel"),
        name="dwconv",
    )(u, w_pad, dw_b.reshape(1, ch))


N_RANK = PEER_TOPK + 1


def _top_values(s, count):
    vals = []
    for _ in range(count):
        m = jnp.max(s, axis=0, keepdims=True)
        vals.append(m)
        s = jnp.where(s == m, NEG_INF, s)
    return vals


def _peer_topk(q, keys, *, tt):
    n_tok = q.shape[0]
    pairs = [(a, b) for a in range(N_RANK) for b in range(N_RANK) if (a + 1) * (b + 1) <= N_RANK]

    n_cand = -(-len(pairs) // 8) * 8

    def body(q_ref, keys_ref, ab_ref, thr_ref, cand_scr):
        cand_scr[...] = jnp.full(cand_scr.shape, NEG_INF, F32)

        def head(h, carry):
            raw, top = [], []
            for p in range(2):
                col = pl.multiple_of((2 * h + p) * PEER_NK, PEER_NK)
                s = _dot_nt(keys_ref[p].astype(BF16), q_ref[:, pl.ds(col, PEER_NK)])
                raw.append(s)
                top.append(_top_values(s, N_RANK))
            for idx, (a, b) in enumerate(pairs):
                cand_scr[idx:idx + 1, :] = top[0][a] + top[1][b]
            best = _top_values(cand_scr[...], N_RANK)
            z = jnp.ones_like(best[0])
            for i in range(1, PEER_TOPK):
                z = z + jnp.exp(best[i] - best[0])
            log_z = jnp.log(z)
            mid = 0.5 * (best[PEER_TOPK - 1] + best[PEER_TOPK])
            ab_ref[2 * h] = (raw[0] - top[0][0]) * LOG2E
            ab_ref[2 * h + 1] = (raw[1] - top[1][0] - log_z) * LOG2E
            thr_ref[pl.ds(h, 1), :] = (mid - best[0] - log_z) * LOG2E
            return carry

        lax.fori_loop(0, PEER_HEADS, head, 0)

    return pl.pallas_call(
        body,
        grid=(n_tok // tt,),
        in_specs=[
            pl.BlockSpec((tt, q.shape[1]), lambda i: (i, 0)),
            pl.BlockSpec(keys.shape, lambda i: (0, 0, 0)),
        ],
        out_specs=[
            pl.BlockSpec((2 * PEER_HEADS, PEER_NK, tt), lambda i: (0, 0, i)),
            pl.BlockSpec((PEER_HEADS, tt), lambda i: (0, i)),
        ],
        out_shape=[
            jax.ShapeDtypeStruct((2 * PEER_HEADS, PEER_NK, n_tok), F32),
            jax.ShapeDtypeStruct((PEER_HEADS, n_tok), F32),
        ],
        scratch_shapes=[pltpu.VMEM((n_cand, tt), F32)],
        compiler_params=_params("parallel"),
        name="peer_topk",
    )(q, keys)


def _peer_dense(xt, u_tab, v_tab, ab, thr, res, gate, *, mod_row, tt, et, pipelined, pieces, lane_chunk=128):
    d, n_tok = xt.shape
    n_exp = u_tab.shape[0]
    rows_per_step = et // PEER_NK
    n_blk = n_exp // et
    lag_b, lag_c = (1, 2) if pipelined else (0, 0)
    n_steps = n_blk + lag_c
    n_lane = tt // lane_chunk
    a_rows = et // pieces
    c_cols = d // pieces
    chunks = [(r, c) for r in range(rows_per_step) for c in range(n_lane)]
    per_group = len(chunks) // (2 * pieces)
    assert 8 % rows_per_step == 0 and len(chunks) % (2 * pieces) == 0

    def body(xt_ref, u_ref, v_ref, ab_ref, thr_ref, res_ref, gate_ref, o_ref, acc_scr, *slots):
        s = pl.program_id(1)
        act_scr, w_scr = slots[:len(slots) // 2], slots[len(slots) // 2:]

        @pl.when(s == 0)
        def _():
            acc_scr[...] = jnp.zeros_like(acc_scr)
            if pipelined:
                for ref in slots:
                    ref[...] = jnp.zeros_like(ref)

        def step(cur, prev):
            blk_b = jnp.clip(s - lag_b, 0, n_blk - 1)
            grp = pl.multiple_of((blk_b * rows_per_step) // 8 * 8, 8)
            sub = (blk_b * rows_per_step) % 8 if not pipelined else (prev * rows_per_step) % 8

            def stage_a(q):
                qs = slice(q * a_rows, (q + 1) * a_rows)
                act_scr[cur][qs, :] = _dot(u_ref[qs, :], xt_ref[...])

            def stage_b(r, c):
                rs = slice(r * PEER_NK, (r + 1) * PEER_NK)
                cs = slice(c * lane_chunk, (c + 1) * lane_chunk)
                g = jnp.zeros((PEER_NK, lane_chunk), F32)
                for h in range(PEER_HEADS):
                    a_rows8 = ab_ref[2 * h, pl.ds(grp, 8), cs]
                    if pipelined:
                        a_row = a_rows8[sub + r:sub + r + 1]
                    else:
                        a_row = jnp.max(jnp.where(iota8 == sub + r, a_rows8, NEG_INF), axis=0, keepdims=True)
                    sc = a_row + ab_ref[2 * h + 1, :, cs]
                    g = g + jnp.where(sc >= thr_ref[h:h + 1, cs], jnp.exp2(sc), 0.0)
                a = act_scr[prev][rs, cs]
                gelu = 0.5 * a * (1.0 + lax.erf(a * SQRT_HALF))
                w_scr[prev][rs, cs] = (g * gelu).astype(BF16)

            def stage_c(q):
                ds = slice(q * c_cols, (q + 1) * c_cols)
                acc_scr[:, ds] += lax.dot_general(w_scr[cur][...], v_ref[:, ds], (((0,), (0,)), ((), ())),
                                                  preferred_element_type=F32)

            if not pipelined:
                iota8 = lax.broadcasted_iota(jnp.int32, (8, lane_chunk), 0)
                stage_a(0)
                for r, c in chunks:
                    stage_b(r, c)
                stage_c(0)
                return
            groups = [chunks[i * per_group:(i + 1) * per_group] for i in range(2 * pieces)]
            for q in range(pieces):
                stage_a(q)
                for r, c in groups[2 * q]:
                    stage_b(r, c)
                stage_c(q)
                for r, c in groups[2 * q + 1]:
                    stage_b(r, c)

        if pipelined:
            @pl.when(s % 2 == 0)
            def _():
                step(0, 1)

            @pl.when(s % 2 == 1)
            def _():
                step(1, 0)
        else:
            step(0, 0)

        @pl.when(s == n_steps - 1)
        def _():
            o_ref[...] = res_ref[...] + gate_ref[...] * acc_scr[...]

    n_slot = 2 if pipelined else 1
    return pl.pallas_call(
        body,
        grid=(n_tok // tt, n_steps),
        in_specs=[
            pl.BlockSpec((d, tt), lambda i, s: (0, i)),
            pl.BlockSpec((et, d), lambda i, s: (jnp.minimum(s, n_blk - 1), 0)),
            pl.BlockSpec((et, d), lambda i, s: (jnp.maximum(s - lag_c, 0), 0)),
            pl.BlockSpec((2 * PEER_HEADS, PEER_NK, tt), lambda i, s: (0, 0, i)),
            pl.BlockSpec((PEER_HEADS, tt), lambda i, s: (0, i)),
            pl.BlockSpec((tt, d), lambda i, s: (i, 0)),
            pl.BlockSpec((None, 1, d), lambda i, s: (mod_row(i), 0, 0)),
        ],
        out_specs=pl.BlockSpec((tt, d), lambda i, s: (i, 0)),
        out_shape=jax.ShapeDtypeStruct((n_tok, d), F32),
        scratch_shapes=([pltpu.VMEM((tt, d), F32)] + [pltpu.VMEM((et, tt), F32)] * n_slot
                        + [pltpu.VMEM((et, tt), BF16)] * n_slot),
        compiler_params=_params("parallel", "arbitrary"),
        name="peer_dense",
    )(xt, u_tab, v_tab, ab, thr, res, gate)


def _final_norm(x, g, *, tm):
    n_tok, d = x.shape

    def body(x_ref, g_ref, o_ref):
        xv = x_ref[...]
        o_ref[...] = xv * lax.rsqrt(jnp.mean(xv * xv, axis=-1, keepdims=True) + NORM_EPS) * g_ref[...]

    return pl.pallas_call(
        body,
        grid=(n_tok // tm,),
        in_specs=[pl.BlockSpec((tm, d), lambda i: (i, 0)), pl.BlockSpec((1, d), lambda i: (0, 0))],
        out_specs=pl.BlockSpec((tm, d), lambda i: (i, 0)),
        out_shape=jax.ShapeDtypeStruct((n_tok, d), F32),
        compiler_params=_params("parallel"),
        name="final_norm",
    )(x, g.reshape(1, d))


TM = 1024
PEER_TT = 512
PEER_ET = 512
TOPK_TT = 256


def kernel(x_prompt, x_sample, cache_k, cache_v, c, c_ctx, norm1_g, norm2_g, final_g,
           ada_w, ada_b, w_in_e, w_out_e, lam_q1, lam_k1, lam_q2, lam_k2, subln_g,
           pool_w, pool_scale, pw1_w, pw1_b, dw_w, dw_b, cln_g, cln_b, pw2_w, pw2_b,
           peer_wq, peer_keys, peer_u, peer_v):
    d = D_MODEL
    bp, tp, _ = x_prompt.shape
    bs, ts, _ = x_sample.shape

    cvec = jnp.zeros((MOD_ROWS, d), F32).at[0].set(c_ctx).at[1:1 + bs].set(c)
    mods = _ada(cvec, ada_w, ada_b)
    mods = mods.reshape(DEPTH, MOD_ROWS, 6, 1, d).transpose(0, 2, 1, 3, 4)

    def sample_row(tile):
        return lambda i: 1 + (i * tile) // ts

    streams = [
        dict(x=x_prompt.reshape(bp * tp, d), b=bp, t=tp, mod_row=lambda tile: (lambda i: 0)),
        dict(x=x_sample.reshape(bs * ts, d), b=bs, t=ts, mod_row=sample_row),
    ]
    rope = _rope_tables(ts)
    new_k, new_v = [], []
    peer_variant = {(0, 0): (True, 4), (0, 1): (False, 1), (1, 0): (True, 1), (1, 1): (True, 2)}

    for i in range(DEPTH):
        j = i // 2
        sh1, sc1, g1, sh2, sc2, g2 = (mods[i, m] for m in range(6))
        wq = peer_wq[i].astype(BF16)
        u_tab = peer_u[i].astype(BF16)
        v_tab = peer_v[i].astype(BF16)
        if i % 2 == 0:
            lam_init = 0.8 - 0.6 * math.exp(-0.3 * i)
            w_in = w_in_e[j].astype(BF16)
            w_out = w_out_e[j].astype(BF16)
            pw = pool_w[j].astype(BF16)
            lamv = jnp.zeros((8, LANES), F32)
            lamv = lamv.at[0, :A_HD].set(lam_q1[j]).at[1, :A_HD].set(lam_k1[j])
            lamv = lamv.at[2, :A_HD].set(lam_q2[j]).at[3, :A_HD].set(lam_k2[j])
        else:
            w1 = pw1_w[j].astype(BF16)
            w2 = pw2_w[j].astype(BF16)

        for si, st in enumerate(streams):
            x, mod_row = st["x"], st["mod_row"](TM)
            if i % 2 == 0:
                proj = _linear([x], w_in, mod_row=mod_row, tm=TM, tn=1024, prologue="rms_mod",
                               norm_g=norm1_g[i], shift=sh1, scale=sc1, name="w_in")
                proj3 = proj.reshape(st["b"], st["t"], IN_EVEN)
                if si == 0:
                    new_k.append(proj3[:, :, A_QK:2 * A_QK].reshape(bp, tp, A_HEADS, 2, A_HD))
                    new_v.append(proj3[:, :, 2 * A_QK:2 * A_QK + A_V].reshape(bp, tp, A_HEADS, 2 * A_HD))
                    att = _attention(proj3, lamv, subln_g[j], lam_init, tq=tp)
                else:
                    ctx = (cache_k[:, j].reshape(bs, -1, A_QK), cache_v[:, j].reshape(bs, -1, A_V))
                    att = _attention(proj3, lamv, subln_g[j], lam_init, tq=512, ctx=ctx, rope=rope)
                pooled = _pool(proj3, pw, pool_scale[j])
                x = _linear([att.reshape(-1, A_V), pooled.reshape(-1, POOL_WIDTH)], w_out,
                            mod_row=mod_row, tm=TM, tn=1024, epilogue="resid", res=x, gate=g1,
                            name="w_out")
            else:
                u = _linear([x], w1, mod_row=mod_row, tm=TM, tn=1024, prologue="rms_mod",
                            epilogue="glu", bias=pw1_b[j], norm_g=norm1_g[i], shift=sh1, scale=sc1,
                            out_dtype=BF16, name="pw1_glu")
                u = _dwconv(u.reshape(st["b"], st["t"], d), dw_w[j], dw_b[j]).reshape(-1, d)
                x = _linear([u], w2, mod_row=mod_row, tm=TM, tn=1024, prologue="ln_silu",
                            epilogue="resid", bias=pw2_b[j], norm_g=cln_g[j], norm_b=cln_b[j],
                            res=x, gate=g1, name="pw2")

            q, xt = _linear([x], wq, mod_row=mod_row, tm=TM, tn=1024, prologue="rms_mod",
                            norm_g=norm2_g[i], shift=sh2, scale=sc2, emit_xt=True, out_dtype=BF16,
                            name="peer_q")
            ab, thr = _peer_topk(q, peer_keys[i], tt=TOPK_TT)
            pipelined, pieces = peer_variant[(i, si)]
            x = _peer_dense(xt, u_tab, v_tab, ab, thr, x, g2, mod_row=st["mod_row"](PEER_TT),
                            tt=PEER_TT, et=PEER_ET, pipelined=pipelined, pieces=pieces)
            st["x"] = x

    y_prompt = _final_norm(streams[0]["x"], final_g, tm=TM).reshape(bp, tp, d)
    y_sample = _final_norm(streams[1]["x"], final_g, tm=TM).reshape(bs, ts, d)
    return (y_prompt, y_sample, jnp.stack(new_k, axis=1), jnp.stack(new_v, axis=1))
```

```python
import math

import jax
import jax.numpy as jnp
from jax import lax
from jax.experimental import pallas as pl
from jax.experimental.pallas import tpu as pltpu

F32 = jnp.float32
BF16 = jnp.bfloat16

D_MODEL = 2048
DEPTH = 2
GRID_W = 64
A_HD = 64
A_HEADS = D_MODEL // (4 * A_HD)
A_QK = A_HEADS * 2 * A_HD
A_V = A_HEADS * 2 * A_HD
POOL_WIDTH = D_MODEL // 2
POOL_WINDOWS = (2, 4, 8, 16)
POOL_C = POOL_WIDTH // len(POOL_WINDOWS)
IN_EVEN = 2 * A_QK + A_V + POOL_WIDTH
CONV_K = 31
PEER_HEADS = 8
PEER_NK = 128
PEER_N = PEER_NK * PEER_NK
PEER_TOPK = 16
ROPE_THETA = 10000.0
ROPE_AXIS_F = A_HD // 4
NORM_EPS = 1e-6
LN_EPS = 1e-5

LANES = 128
MOD_ROWS = 16
VMEM_LIMIT = 56 * 1024 * 1024
HALO = 16
LOG2E = 1.4426950408889634
SQRT_HALF = 0.7071067811865476
NEG_INF = float("-inf")


def _params(*sem):
    return pltpu.CompilerParams(dimension_semantics=sem, vmem_limit_bytes=VMEM_LIMIT)


def _dot(a, b):
    return jnp.dot(a, b, preferred_element_type=F32)


def _dot_nt(a, b):
    return lax.dot_general(a, b, (((1,), (1,)), ((), ())), preferred_element_type=F32)


def _ada_body(c_ref, w_ref, b_ref, o_ref):
    c = c_ref[...]
    s = (c * jax.nn.sigmoid(c)).astype(BF16)
    o_ref[...] = _dot(s, w_ref[...].astype(BF16)) + b_ref[...]


def _ada(cvec, ada_w, ada_b):
    depth, d, n = ada_w.shape
    tn = 1024
    return pl.pallas_call(
        _ada_body,
        grid=(depth, n // tn),
        in_specs=[
            pl.BlockSpec((MOD_ROWS, d), lambda l, j: (0, 0)),
            pl.BlockSpec((None, d, tn), lambda l, j: (l, 0, j)),
            pl.BlockSpec((None, 1, tn), lambda l, j: (l, 0, j)),
        ],
        out_specs=pl.BlockSpec((None, MOD_ROWS, tn), lambda l, j: (l, 0, j)),
        out_shape=jax.ShapeDtypeStruct((depth, MOD_ROWS, n), F32),
        compiler_params=_params("parallel", "parallel"),
        name="ada",
    )(cvec, ada_w, ada_b.reshape(depth, 1, n))


def _linear(xs, w, *, mod_row, tm, tn, prologue="none", epilogue="none", bias=None,
            norm_g=None, norm_b=None, shift=None, scale=None, res=None, gate=None,
            emit_xt=False, out_dtype=F32, name="linear"):
    n_tok = xs[0].shape[0]
    k_tot = sum(x.shape[1] for x in xs)
    n_w = w.shape[1]
    n_out = n_w // 2 if epilogue == "glu" else n_w
    nj = n_out // tn
    row_map = lambda i, j: (i, 0)
    vec_map = lambda i, j: (0, 0)
    mod_map = lambda i, j: (mod_row(i), 0, 0)

    args, specs = [], []
    for x in xs:
        args.append(x)
        specs.append(pl.BlockSpec((tm, x.shape[1]), row_map))
    if prologue == "rms_mod":
        args += [norm_g.reshape(1, k_tot), shift, scale]
        specs += [pl.BlockSpec((1, k_tot), vec_map),
                  pl.BlockSpec((None, 1, k_tot), mod_map),
                  pl.BlockSpec((None, 1, k_tot), mod_map)]
    elif prologue == "ln_silu":
        args += [norm_g.reshape(1, k_tot), norm_b.reshape(1, k_tot)]
        specs += [pl.BlockSpec((1, k_tot), vec_map)] * 2
    args.append(w)
    specs.append(pl.BlockSpec((k_tot, tn), lambda i, j: (0, j)))
    if epilogue == "glu":
        args.append(w)
        specs.append(pl.BlockSpec((k_tot, tn), lambda i, j: (0, j + nj)))
    if bias is not None:
        b2 = bias.reshape(1, n_w)
        args.append(b2)
        specs.append(pl.BlockSpec((1, tn), lambda i, j: (0, j)))
        if epilogue == "glu":
            args.append(b2)
            specs.append(pl.BlockSpec((1, tn), lambda i, j: (0, j + nj)))
    if epilogue == "resid":
        args += [res, gate]
        specs += [pl.BlockSpec((tm, tn), lambda i, j: (i, j)),
                  pl.BlockSpec((None, 1, tn), lambda i, j: (mod_row(i), 0, j))]

    out_shape = [jax.ShapeDtypeStruct((n_tok, n_out), out_dtype)]
    out_specs = [pl.BlockSpec((tm, tn), lambda i, j: (i, j))]
    if emit_xt:
        out_shape.append(jax.ShapeDtypeStruct((k_tot, n_tok), BF16))
        out_specs.append(pl.BlockSpec((k_tot, tm), lambda i, j: (0, i)))
    n_x = len(xs)
    has_bias = bias is not None

    def body(*refs):
        it = iter(refs)
        x_refs = [next(it) for _ in range(n_x)]
        if prologue == "rms_mod":
            g_ref, sh_ref, sc_ref = next(it), next(it), next(it)
        elif prologue == "ln_silu":
            g_ref, b_ref = next(it), next(it)
        w_ref = next(it)
        w2_ref = next(it) if epilogue == "glu" else None
        bias_ref = next(it) if has_bias else None
        bias2_ref = next(it) if (has_bias and epilogue == "glu") else None
        if epilogue == "resid":
            res_ref, gate_ref = next(it), next(it)
        o_ref = next(it)
        xt_ref = next(it) if emit_xt else None
        xm_scr = next(it)

        @pl.when(pl.program_id(1) == 0)
        def _():
            if prologue == "none":
                off = 0
                for xr in x_refs:
                    kk = xr.shape[1]
                    xm_scr[:, off:off + kk] = xr[...].astype(BF16)
                    off += kk
                return
            x = x_refs[0][...].astype(F32)
            if prologue == "rms_mod":
                y = x * lax.rsqrt(jnp.mean(x * x, axis=-1, keepdims=True) + NORM_EPS)
                y = y * g_ref[...]
                y = y * (1.0 + sc_ref[...]) + sh_ref[...]
            else:
                mu = jnp.mean(x, axis=-1, keepdims=True)
                xc = x - mu
                y = xc * lax.rsqrt(jnp.mean(xc * xc, axis=-1, keepdims=True) + LN_EPS)
                y = y * g_ref[...] + b_ref[...]
                y = y * jax.nn.sigmoid(y)
            xm_scr[...] = y.astype(BF16)
            if emit_xt:
                xt_ref[...] = y.T.astype(BF16)

        xm = xm_scr[...]
        y = _dot(xm, w_ref[...])
        if has_bias:
            y = y + bias_ref[...]
        if epilogue == "glu":
            gte = _dot(xm, w2_ref[...])
            if has_bias:
                gte = gte + bias2_ref[...]
            y = y * jax.nn.sigmoid(gte)
        elif epilogue == "resid":
            y = res_ref[...] + gate_ref[...] * y
        o_ref[...] = y.astype(o_ref.dtype)

    outs = pl.pallas_call(
        body,
        grid=(n_tok // tm, nj),
        in_specs=specs,
        out_specs=out_specs,
        out_shape=out_shape,
        scratch_shapes=[pltpu.VMEM((tm, k_tot), BF16)],
        compiler_params=_params("parallel", "arbitrary"),
        name=name,
    )(*args)
    return outs if emit_xt else outs[0]


def _rope_tables(n_tokens):
    t = jnp.arange(n_tokens)
    pos = jnp.stack([(t // GRID_W).astype(F32), (t % GRID_W).astype(F32)], axis=1)
    freq = ROPE_THETA ** (-jnp.arange(ROPE_AXIS_F, dtype=F32) / ROPE_AXIS_F)
    lane = jnp.arange(LANES)
    axis = (lane % A_HD) // (2 * ROPE_AXIS_F)
    half = (lane // ROPE_AXIS_F) % 2
    ang = pos[:, axis] * freq[lane % ROPE_AXIS_F][None, :]
    cos, sin = jnp.cos(ang), jnp.sin(ang)
    sin_lo = jnp.where(half[None, :] == 1, sin, 0.0)
    sin_hi = jnp.where(half[None, :] == 0, -sin, 0.0)
    return cos, sin_lo, sin_hi


def _rope(x, cos, sin_lo, sin_hi):
    return (x * cos + pltpu.roll(x, ROPE_AXIS_F, 1) * sin_lo
            + pltpu.roll(x, LANES - ROPE_AXIS_F, 1) * sin_hi)


def _attention(proj, lamv, subln_g, lam_init, *, tq, ctx=None, rope=None):
    bsz, t_len, _ = proj.shape
    nq = t_len // tq
    has_ctx = ctx is not None
    out_scale = 1.0 - lam_init

    def body(*refs):
        it = iter(refs)
        lam_ref, g_ref, q_ref, k_ref, v_ref = (next(it) for _ in range(5))
        if has_ctx:
            ck_ref, cv_ref = next(it), next(it)
            cq_ref, slq_ref, shq_ref, ck_tab, slk_tab, shk_tab = (next(it) for _ in range(6))
        o_ref = next(it)
        if has_ctx:
            kr_scr = next(it)

            @pl.when(pl.program_id(2) == 0)
            def _():
                kr_scr[...] = _rope(k_ref[...], ck_tab[...], slk_tab[...], shk_tab[...]).astype(BF16)

        lv = lam_ref[...]
        lam = (jnp.exp(jnp.sum(lv[0:1] * lv[1:2], axis=-1, keepdims=True))
               - jnp.exp(jnp.sum(lv[2:3] * lv[3:4], axis=-1, keepdims=True)) + lam_init)
        q = q_ref[...]
        if has_ctx:
            q = _rope(q, cq_ref[...], slq_ref[...], shq_ref[...])
            k_own = kr_scr[...]
        else:
            k_own = k_ref[...].astype(BF16)
        q = q * (A_HD ** -0.5 * LOG2E)
        lane = lax.broadcasted_iota(jnp.int32, q.shape, 1)
        v_own = v_ref[...].astype(BF16)
        if has_ctx:
            k_ctx = ck_ref[...].astype(BF16)
            v_ctx = cv_ref[...].astype(BF16)

        parts = []
        for comp in range(2):
            in_comp = (lane < A_HD) if comp == 0 else (lane >= A_HD)
            qc = jnp.where(in_comp, q, 0.0).astype(BF16)
            s_own = _dot_nt(qc, k_own)
            m = jnp.max(s_own, axis=-1, keepdims=True)
            if has_ctx:
                s_ctx = _dot_nt(qc, k_ctx)
                m = jnp.maximum(m, jnp.max(s_ctx, axis=-1, keepdims=True))
            e_own = jnp.exp2(s_own - m)
            l = jnp.sum(e_own, axis=-1, keepdims=True)
            pv = _dot(e_own.astype(BF16), v_own)
            if has_ctx:
                e_ctx = jnp.exp2(s_ctx - m)
                l = l + jnp.sum(e_ctx, axis=-1, keepdims=True)
                pv = pv + _dot(e_ctx.astype(BF16), v_ctx)
            parts.append(pv * (1.0 / l))
        out = parts[0] - lam * parts[1]
        out = out * lax.rsqrt(jnp.mean(out * out, axis=-1, keepdims=True) + NORM_EPS)
        o_ref[...] = (out * g_ref[...] * out_scale).astype(o_ref.dtype)

    nh = A_HEADS
    args = [lamv, subln_g.reshape(1, 2 * A_HD), proj, proj, proj]
    specs = [
        pl.BlockSpec((8, LANES), lambda b, h, i: (0, 0)),
        pl.BlockSpec((1, 2 * A_HD), lambda b, h, i: (0, 0)),
        pl.BlockSpec((None, tq, LANES), lambda b, h, i: (b, i, h)),
        pl.BlockSpec((None, t_len, LANES), lambda b, h, i: (b, 0, nh + h)),
        pl.BlockSpec((None, t_len, LANES), lambda b, h, i: (b, 0, 2 * nh + h)),
    ]
    scratch = []
    if has_ctx:
        p_len = ctx[0].shape[1]
        args += [ctx[0], ctx[1]]
        specs += [pl.BlockSpec((None, p_len, LANES), lambda b, h, i: (b, 0, h))] * 2
        args += list(rope) + list(rope)
        specs += [pl.BlockSpec((tq, LANES), lambda b, h, i: (i, 0))] * 3
        specs += [pl.BlockSpec((t_len, LANES), lambda b, h, i: (0, 0))] * 3
        scratch = [pltpu.VMEM((t_len, LANES), BF16)]
    return pl.pallas_call(
        body,
        grid=(bsz, nh, nq),
        in_specs=specs,
        out_specs=pl.BlockSpec((None, tq, LANES), lambda b, h, i: (b, i, h)),
        out_shape=jax.ShapeDtypeStruct((bsz, t_len, A_V), BF16),
        scratch_shapes=scratch,
        compiler_params=_params("parallel", "parallel", "arbitrary"),
        name="attention_ctx" if has_ctx else "attention",
    )(*args)


def _pool(proj, pool_w, pool_scale):
    bsz, t_len, n_in = proj.shape
    col_blk = (n_in - POOL_WIDTH) // POOL_WIDTH

    def body(p_ref, w_ref, sc_ref, o_ref, pad_scr):
        zeros = jnp.zeros((HALO, POOL_C), F32)
        pad_scr[0:HALO, :] = zeros
        pad_scr[HALO + t_len:2 * HALO + t_len, :] = zeros
        tok = lax.broadcasted_iota(jnp.int32, (t_len, POOL_C), 0)
        for g, win in enumerate(POOL_WINDOWS):
            cols = slice(g * POOL_C, (g + 1) * POOL_C)
            x = p_ref[:, cols]
            pad_scr[HALO:HALO + t_len, :] = x
            back, fwd = win // 2, win - win // 2 - 1
            total = pad_scr[pl.ds(HALO - back, t_len), :]
            for o in range(-back + 1, fwd + 1):
                total = total + pad_scr[pl.ds(HALO + o, t_len), :]
            lo = jnp.maximum(tok - back, 0)
            hi = jnp.minimum(tok + fwd, t_len - 1)
            cnt = (hi - lo + 1).astype(F32)
            diff = (total / cnt - x).astype(BF16)
            o_ref[:, cols] = (_dot(diff, w_ref[g]) * sc_ref[:, cols]).astype(o_ref.dtype)

    return pl.pallas_call(
        body,
        grid=(bsz,),
        in_specs=[
            pl.BlockSpec((None, t_len, POOL_WIDTH), lambda b: (b, 0, col_blk)),
            pl.BlockSpec(pool_w.shape, lambda b: (0, 0, 0)),
            pl.BlockSpec((1, POOL_WIDTH), lambda b: (0, 0)),
        ],
        out_specs=pl.BlockSpec((None, t_len, POOL_WIDTH), lambda b: (b, 0, 0)),
        out_shape=jax.ShapeDtypeStruct((bsz, t_len, POOL_WIDTH), BF16),
        scratch_shapes=[pltpu.VMEM((t_len + 2 * HALO, POOL_C), F32)],
        compiler_params=_params("parallel"),
        name="pool",
    )(proj, pool_w, pool_scale.reshape(1, POOL_WIDTH))


def _dwconv(u, dw_w, dw_b, *, cb=256, rows=64):
    bsz, t_len, ch = u.shape
    taps = dw_w.shape[0]
    reach = taps // 2
    assert reach <= HALO and taps <= 32
    ext = t_len + 2 * HALO - 8

    def body(u_ref, w_ref, b_ref, o_ref, pad_scr, sh_scr):
        zeros = jnp.zeros((HALO, cb), F32)
        pad_scr[0:HALO, :] = zeros
        pad_scr[HALO + t_len:2 * HALO + t_len, :] = zeros
        pad_scr[HALO:HALO + t_len, :] = u_ref[...].astype(F32)
        for b in range(8):
            off = b + HALO - reach
            sh_scr[b] = pad_scr[off:off + ext, :]

        def chunk(r, carry):
            base = pl.multiple_of(r * rows, rows)
            acc = jnp.zeros((rows, cb), F32)
            for k in range(taps):
                a, b = divmod(k, 8)
                acc = acc + sh_scr[b, pl.ds(base + 8 * a, rows), :] * w_ref[k:k + 1, :]
            o_ref[pl.ds(base, rows), :] = (acc + b_ref[...]).astype(o_ref.dtype)
            return carry

        lax.fori_loop(0, t_len // rows, chunk, 0)

    w_pad = jnp.zeros((32, ch), F32).at[:taps].set(dw_w)
    return pl.pallas_call(
        body,
        grid=(bsz, ch // cb),
        in_specs=[
            pl.BlockSpec((None, t_len, cb), lambda b, c: (b, 0, c)),
            pl.BlockSpec((32, cb), lambda b, c: (0, c)),
            pl.BlockSpec((1, cb), lambda b, c: (0, c)),
        ],
        out_specs=pl.BlockSpec((None, t_len, cb), lambda b, c: (b, 0, c)),
        out_shape=jax.ShapeDtypeStruct((bsz, t_len, ch), u.dtype),
        scratch_shapes=[pltpu.VMEM((t_len + 2 * HALO, cb), F32), pltpu.VMEM((8, ext, cb), F32)],
        compiler_params=_params("parallel", "parallel"),
        name="dwconv",
    )(u, w_pad, dw_b.reshape(1, ch))


N_RANK = PEER_TOPK + 1


def _top_values(s, count):
    vals = []
    for _ in range(count):
        m = jnp.max(s, axis=0, keepdims=True)
        vals.append(m)
        s = jnp.where(s == m, NEG_INF, s)
    return vals


def _sorting_network(n):
    pairs = []
    p = 1
    while p < n:
        k = p
        while k >= 1:
            for j in range(k % p, n - k, 2 * k):
                for i in range(min(k, n - j - k)):
                    if (i + j) // (2 * p) == (i + j + k) // (2 * p):
                        pairs.append((i + j, i + j + k))
            k //= 2
        p *= 2
    return pairs


def _top_values_sorted_lists(s, count):
    rows, cols = s.shape
    n_slab = rows // 8
    s3 = s.reshape(n_slab, 8, cols)
    v = [s3[j] for j in range(n_slab)]
    for i, j in _sorting_network(n_slab):
        hi, lo = jnp.maximum(v[i], v[j]), jnp.minimum(v[i], v[j])
        v[i], v[j] = hi, lo
    vals = []
    for a in range(count):
        m = jnp.max(v[0], axis=0, keepdims=True)
        vals.append(m)
        adv = v[0] == m
        for j in range(min(count - 1 - a, n_slab)):
            nxt = v[j + 1] if j + 1 < n_slab else NEG_INF
            v[j] = jnp.where(adv, nxt, v[j])
    return vals


def _peer_topk(q, keys, *, tt):
    n_tok = q.shape[0]
    pairs = [(a, b) for a in range(N_RANK) for b in range(N_RANK) if (a + 1) * (b + 1) <= N_RANK]

    n_cand = -(-len(pairs) // 8) * 8

    def body(q_ref, keys_ref, ab_ref, thr_ref, cand_scr):
        cand_scr[...] = jnp.full(cand_scr.shape, NEG_INF, F32)

        def head(h, carry):
            raw, top = [], []
            for p in range(2):
                col = pl.multiple_of((2 * h + p) * PEER_NK, PEER_NK)
                s = _dot_nt(keys_ref[p].astype(BF16), q_ref[:, pl.ds(col, PEER_NK)])
                raw.append(s)
                top.append(_top_values_sorted_lists(s, N_RANK))
            for idx, (a, b) in enumerate(pairs):
                cand_scr[idx:idx + 1, :] = top[0][a] + top[1][b]
            best = _top_values(cand_scr[...], N_RANK)
            z = jnp.ones_like(best[0])
            for i in range(1, PEER_TOPK):
                z = z + jnp.exp(best[i] - best[0])
            log_z = jnp.log(z)
            mid = 0.5 * (best[PEER_TOPK - 1] + best[PEER_TOPK])
            ab_ref[2 * h] = (raw[0] - top[0][0]) * LOG2E
            ab_ref[2 * h + 1] = (raw[1] - top[1][0] - log_z) * LOG2E
            thr_ref[pl.ds(h, 1), :] = (mid - best[0] - log_z) * LOG2E
            return carry

        lax.fori_loop(0, PEER_HEADS, head, 0)

    return pl.pallas_call(
        body,
        grid=(n_tok // tt,),
        in_specs=[
            pl.BlockSpec((tt, q.shape[1]), lambda i: (i, 0)),
            pl.BlockSpec(keys.shape, lambda i: (0, 0, 0)),
        ],
        out_specs=[
            pl.BlockSpec((2 * PEER_HEADS, PEER_NK, tt), lambda i: (0, 0, i)),
            pl.BlockSpec((PEER_HEADS, tt), lambda i: (0, i)),
        ],
        out_shape=[
            jax.ShapeDtypeStruct((2 * PEER_HEADS, PEER_NK, n_tok), F32),
            jax.ShapeDtypeStruct((PEER_HEADS, n_tok), F32),
        ],
        scratch_shapes=[pltpu.VMEM((n_cand, tt), F32)],
        compiler_params=_params("parallel"),
        name="peer_topk",
    )(q, keys)


def _peer_dense(xt, u_tab, v_tab, layer, ab, thr, res, gate, *, mod_row, tt, et, lane_chunk=128):
    d, n_tok = xt.shape
    n_exp = u_tab.shape[1]
    rows_per_step = et // PEER_NK
    n_blk = n_exp // et
    n_lane = tt // lane_chunk
    assert rows_per_step == 8

    def body(xt_ref, u_ref, v_ref, ab_ref, thr_ref, res_ref, gate_ref, o_ref, acc_scr, act_scr, w_scr):
        s = pl.program_id(1)

        @pl.when(s == 0)
        def _():
            acc_scr[...] = jnp.zeros_like(acc_scr)

        grp = pl.multiple_of(s * rows_per_step, rows_per_step)
        act_scr[...] = _dot(u_ref[...], xt_ref[...])
        for r in range(rows_per_step):
            rs = slice(r * PEER_NK, (r + 1) * PEER_NK)
            for c in range(n_lane):
                cs = slice(c * lane_chunk, (c + 1) * lane_chunk)
                g = jnp.zeros((PEER_NK, lane_chunk), F32)
                for h in range(PEER_HEADS):
                    a_row = ab_ref[2 * h, pl.ds(grp, rows_per_step), cs][r:r + 1]
                    sc = a_row + ab_ref[2 * h + 1, :, cs]
                    g = g + jnp.where(sc >= thr_ref[h:h + 1, cs], jnp.exp2(sc), 0.0)
                a = act_scr[rs, cs]
                gelu = 0.5 * a * (1.0 + lax.erf(a * SQRT_HALF))
                w_scr[rs, cs] = (g * gelu).astype(BF16)
        acc_scr[...] += lax.dot_general(w_scr[...], v_ref[...], (((0,), (0,)), ((), ())),
                                        preferred_element_type=F32)

        @pl.when(s == n_blk - 1)
        def _():
            o_ref[...] = res_ref[...] + gate_ref[...] * acc_scr[...]

    return pl.pallas_call(
        body,
        grid=(n_tok // tt, n_blk),
        in_specs=[
            pl.BlockSpec((d, tt), lambda i, s: (0, i)),
            pl.BlockSpec((None, et, d), lambda i, s: (layer, s, 0)),
            pl.BlockSpec((None, et, d), lambda i, s: (layer, s, 0)),
            pl.BlockSpec((2 * PEER_HEADS, PEER_NK, tt), lambda i, s: (0, 0, i)),
            pl.BlockSpec((PEER_HEADS, tt), lambda i, s: (0, i)),
            pl.BlockSpec((tt, d), lambda i, s: (i, 0)),
            pl.BlockSpec((None, 1, d), lambda i, s: (mod_row(i), 0, 0)),
        ],
        out_specs=pl.BlockSpec((tt, d), lambda i, s: (i, 0)),
        out_shape=jax.ShapeDtypeStruct((n_tok, d), F32),
        scratch_shapes=[pltpu.VMEM((tt, d), F32), pltpu.VMEM((et, tt), F32), pltpu.VMEM((et, tt), BF16)],
        compiler_params=_params("parallel", "arbitrary"),
        name="peer_dense",
    )(xt, u_tab, v_tab, ab, thr, res, gate)


def _cast_tables(tabs, *, rows=1024):
    n_l, n_e, d = tabs[0].shape

    def body(*refs):
        for src, dst in zip(refs[:len(tabs)], refs[len(tabs):]):
            dst[...] = src[...].astype(BF16)

    spec = pl.BlockSpec((None, rows, d), lambda l, i: (l, i, 0))
    return pl.pallas_call(
        body,
        grid=(n_l, n_e // rows),
        in_specs=[spec] * len(tabs),
        out_specs=[spec] * len(tabs),
        out_shape=[jax.ShapeDtypeStruct((n_l, n_e, d), BF16)] * len(tabs),
        compiler_params=_params("parallel", "parallel"),
        name="cast_tables",
    )(*tabs)


def _final_norm(x, g, *, tm):
    n_tok, d = x.shape

    def body(x_ref, g_ref, o_ref):
        xv = x_ref[...]
        o_ref[...] = xv * lax.rsqrt(jnp.mean(xv * xv, axis=-1, keepdims=True) + NORM_EPS) * g_ref[...]

    return pl.pallas_call(
        body,
        grid=(n_tok // tm,),
        in_specs=[pl.BlockSpec((tm, d), lambda i: (i, 0)), pl.BlockSpec((1, d), lambda i: (0, 0))],
        out_specs=pl.BlockSpec((tm, d), lambda i: (i, 0)),
        out_shape=jax.ShapeDtypeStruct((n_tok, d), F32),
        compiler_params=_params("parallel"),
        name="final_norm",
    )(x, g.reshape(1, d))


TM = 1024
PEER_TT = 512
PEER_ET = 1024
TOPK_TT = 256


def kernel(x_prompt, x_sample, cache_k, cache_v, c, c_ctx, norm1_g, norm2_g, final_g,
           ada_w, ada_b, w_in_e, w_out_e, lam_q1, lam_k1, lam_q2, lam_k2, subln_g,
           pool_w, pool_scale, pw1_w, pw1_b, dw_w, dw_b, cln_g, cln_b, pw2_w, pw2_b,
           peer_wq, peer_keys, peer_u, peer_v):
    d = D_MODEL
    bp, tp, _ = x_prompt.shape
    bs, ts, _ = x_sample.shape

    cvec = jnp.zeros((MOD_ROWS, d), F32).at[0].set(c_ctx).at[1:1 + bs].set(c)
    mods = _ada(cvec, ada_w, ada_b)
    mods = mods.reshape(DEPTH, MOD_ROWS, 6, 1, d).transpose(0, 2, 1, 3, 4)

    def sample_row(tile):
        return lambda i: 1 + (i * tile) // ts

    streams = [
        dict(x=x_prompt.reshape(bp * tp, d), b=bp, t=tp, mod_row=lambda tile: (lambda i: 0)),
        dict(x=x_sample.reshape(bs * ts, d), b=bs, t=ts, mod_row=sample_row),
    ]
    rope = _rope_tables(ts)
    new_k, new_v = [], []
    u_all, v_all = _cast_tables([peer_u, peer_v])

    for i in range(DEPTH):
        j = i // 2
        sh1, sc1, g1, sh2, sc2, g2 = (mods[i, m] for m in range(6))
        wq = peer_wq[i].astype(BF16)
        if i % 2 == 0:
            lam_init = 0.8 - 0.6 * math.exp(-0.3 * i)
            w_in = w_in_e[j].astype(BF16)
            w_out = w_out_e[j].astype(BF16)
            pw = pool_w[j].astype(BF16)
            lamv = jnp.zeros((8, LANES), F32)
            lamv = lamv.at[0, :A_HD].set(lam_q1[j]).at[1, :A_HD].set(lam_k1[j])
            lamv = lamv.at[2, :A_HD].set(lam_q2[j]).at[3, :A_HD].set(lam_k2[j])
        else:
            w1 = pw1_w[j].astype(BF16)
            w2 = pw2_w[j].astype(BF16)

        for si, st in enumerate(streams):
            x, mod_row = st["x"], st["mod_row"](TM)
            if i % 2 == 0:
                proj = _linear([x], w_in, mod_row=mod_row, tm=TM, tn=1024, prologue="rms_mod",
                               norm_g=norm1_g[i], shift=sh1, scale=sc1, name="w_in")
                proj3 = proj.reshape(st["b"], st["t"], IN_EVEN)
                if si == 0:
                    new_k.append(proj3[:, :, A_QK:2 * A_QK].reshape(bp, tp, A_HEADS, 2, A_HD))
                    new_v.append(proj3[:, :, 2 * A_QK:2 * A_QK + A_V].reshape(bp, tp, A_HEADS, 2 * A_HD))
                    att = _attention(proj3, lamv, subln_g[j], lam_init, tq=tp)
                else:
                    ctx = (cache_k[:, j].reshape(bs, -1, A_QK), cache_v[:, j].reshape(bs, -1, A_V))
                    att = _attention(proj3, lamv, subln_g[j], lam_init, tq=512, ctx=ctx, rope=rope)
                pooled = _pool(proj3, pw, pool_scale[j])
                x = _linear([att.reshape(-1, A_V), pooled.reshape(-1, POOL_WIDTH)], w_out,
                            mod_row=mod_row, tm=TM, tn=1024, epilogue="resid", res=x, gate=g1,
                            name="w_out")
            else:
                u = _linear([x], w1, mod_row=mod_row, tm=TM, tn=1024, prologue="rms_mod",
                            epilogue="glu", bias=pw1_b[j], norm_g=norm1_g[i], shift=sh1, scale=sc1,
                            out_dtype=BF16, name="pw1_glu")
                u = _dwconv(u.reshape(st["b"], st["t"], d), dw_w[j], dw_b[j]).reshape(-1, d)
                x = _linear([u], w2, mod_row=mod_row, tm=TM, tn=1024, prologue="ln_silu",
                            epilogue="resid", bias=pw2_b[j], norm_g=cln_g[j], norm_b=cln_b[j],
                            res=x, gate=g1, name="pw2")

            q, xt = _linear([x], wq, mod_row=mod_row, tm=TM, tn=1024, prologue="rms_mod",
                            norm_g=norm2_g[i], shift=sh2, scale=sc2, emit_xt=True, out_dtype=BF16,
                            name="peer_q")
            ab, thr = _peer_topk(q, peer_keys[i], tt=TOPK_TT)
            x = _peer_dense(xt, u_all, v_all, i, ab, thr, x, g2, mod_row=st["mod_row"](PEER_TT),
                            tt=PEER_TT, et=PEER_ET)
            st["x"] = x

    y_prompt = _final_norm(streams[0]["x"], final_g, tm=TM).reshape(bp, tp, d)
    y_sample = _final_norm(streams[1]["x"], final_g, tm=TM).reshape(bs, ts, d)
    return (y_prompt, y_sample, jnp.stack(new_k, axis=1), jnp.stack(new_v, axis=1))
```

```python
import math

import jax
import jax.numpy as jnp
from jax import lax
from jax.experimental import pallas as pl
from jax.experimental.pallas import tpu as pltpu

F32 = jnp.float32
BF16 = jnp.bfloat16

D_MODEL = 2048
DEPTH = 2
GRID_W = 64
A_HD = 64
A_HEADS = D_MODEL // (4 * A_HD)
A_QK = A_HEADS * 2 * A_HD
A_V = A_HEADS * 2 * A_HD
POOL_WIDTH = D_MODEL // 2
POOL_WINDOWS = (2, 4, 8, 16)
POOL_C = POOL_WIDTH // len(POOL_WINDOWS)
IN_EVEN = 2 * A_QK + A_V + POOL_WIDTH
CONV_K = 31
PEER_HEADS = 8
PEER_NK = 128
PEER_N = PEER_NK * PEER_NK
PEER_TOPK = 16
ROPE_THETA = 10000.0
ROPE_AXIS_F = A_HD // 4
NORM_EPS = 1e-6
LN_EPS = 1e-5

LANES = 128
MOD_ROWS = 16
VMEM_LIMIT = 56 * 1024 * 1024
HALO = 16
LOG2E = 1.4426950408889634
SQRT_HALF = 0.7071067811865476
NEG_INF = float("-inf")


def _params(*sem):
    return pltpu.CompilerParams(dimension_semantics=sem, vmem_limit_bytes=VMEM_LIMIT)


def _dot(a, b):
    return jnp.dot(a, b, preferred_element_type=F32)


def _dot_nt(a, b):
    return lax.dot_general(a, b, (((1,), (1,)), ((), ())), preferred_element_type=F32)


def _ada_body(c_ref, w_ref, b_ref, o_ref):
    c = c_ref[...]
    s = (c * jax.nn.sigmoid(c)).astype(BF16)
    o_ref[...] = _dot(s, w_ref[...].astype(BF16)) + b_ref[...]


def _ada(cvec, ada_w, ada_b):
    depth, d, n = ada_w.shape
    tn = 1024
    return pl.pallas_call(
        _ada_body,
        grid=(depth, n // tn),
        in_specs=[
            pl.BlockSpec((MOD_ROWS, d), lambda l, j: (0, 0)),
            pl.BlockSpec((None, d, tn), lambda l, j: (l, 0, j)),
            pl.BlockSpec((None, 1, tn), lambda l, j: (l, 0, j)),
        ],
        out_specs=pl.BlockSpec((None, MOD_ROWS, tn), lambda l, j: (l, 0, j)),
        out_shape=jax.ShapeDtypeStruct((depth, MOD_ROWS, n), F32),
        compiler_params=_params("parallel", "parallel"),
        name="ada",
    )(cvec, ada_w, ada_b.reshape(depth, 1, n))


def _linear(xs, w, *, mod_row, tm, tn, prologue="none", epilogue="none", bias=None,
            norm_g=None, norm_b=None, shift=None, scale=None, res=None, gate=None,
            emit_xt=False, out_dtype=F32, name="linear"):
    n_tok = xs[0].shape[0]
    k_tot = sum(x.shape[1] for x in xs)
    n_w = w.shape[1]
    n_out = n_w // 2 if epilogue == "glu" else n_w
    nj = n_out // tn
    row_map = lambda i, j: (i, 0)
    vec_map = lambda i, j: (0, 0)
    mod_map = lambda i, j: (mod_row(i), 0, 0)

    args, specs = [], []
    for x in xs:
        args.append(x)
        specs.append(pl.BlockSpec((tm, x.shape[1]), row_map))
    if prologue == "rms_mod":
        args += [norm_g.reshape(1, k_tot), shift, scale]
        specs += [pl.BlockSpec((1, k_tot), vec_map),
                  pl.BlockSpec((None, 1, k_tot), mod_map),
                  pl.BlockSpec((None, 1, k_tot), mod_map)]
    elif prologue == "ln_silu":
        args += [norm_g.reshape(1, k_tot), norm_b.reshape(1, k_tot)]
        specs += [pl.BlockSpec((1, k_tot), vec_map)] * 2
    args.append(w)
    specs.append(pl.BlockSpec((k_tot, tn), lambda i, j: (0, j)))
    if epilogue == "glu":
        args.append(w)
        specs.append(pl.BlockSpec((k_tot, tn), lambda i, j: (0, j + nj)))
    if bias is not None:
        b2 = bias.reshape(1, n_w)
        args.append(b2)
        specs.append(pl.BlockSpec((1, tn), lambda i, j: (0, j)))
        if epilogue == "glu":
            args.append(b2)
            specs.append(pl.BlockSpec((1, tn), lambda i, j: (0, j + nj)))
    if epilogue == "resid":
        args += [res, gate]
        specs += [pl.BlockSpec((tm, tn), lambda i, j: (i, j)),
                  pl.BlockSpec((None, 1, tn), lambda i, j: (mod_row(i), 0, j))]

    out_shape = [jax.ShapeDtypeStruct((n_tok, n_out), out_dtype)]
    out_specs = [pl.BlockSpec((tm, tn), lambda i, j: (i, j))]
    if emit_xt:
        out_shape.append(jax.ShapeDtypeStruct((k_tot, n_tok), BF16))
        out_specs.append(pl.BlockSpec((k_tot, tm), lambda i, j: (0, i)))
    n_x = len(xs)
    has_bias = bias is not None

    def body(*refs):
        it = iter(refs)
        x_refs = [next(it) for _ in range(n_x)]
        if prologue == "rms_mod":
            g_ref, sh_ref, sc_ref = next(it), next(it), next(it)
        elif prologue == "ln_silu":
            g_ref, b_ref = next(it), next(it)
        w_ref = next(it)
        w2_ref = next(it) if epilogue == "glu" else None
        bias_ref = next(it) if has_bias else None
        bias2_ref = next(it) if (has_bias and epilogue == "glu") else None
        if epilogue == "resid":
            res_ref, gate_ref = next(it), next(it)
        o_ref = next(it)
        xt_ref = next(it) if emit_xt else None
        xm_scr = next(it)

        @pl.when(pl.program_id(1) == 0)
        def _():
            if prologue == "none":
                off = 0
                for xr in x_refs:
                    kk = xr.shape[1]
                    xm_scr[:, off:off + kk] = xr[...].astype(BF16)
                    off += kk
                return
            x = x_refs[0][...].astype(F32)
            if prologue == "rms_mod":
                y = x * lax.rsqrt(jnp.mean(x * x, axis=-1, keepdims=True) + NORM_EPS)
                y = y * g_ref[...]
                y = y * (1.0 + sc_ref[...]) + sh_ref[...]
            else:
                mu = jnp.mean(x, axis=-1, keepdims=True)
                xc = x - mu
                y = xc * lax.rsqrt(jnp.mean(xc * xc, axis=-1, keepdims=True) + LN_EPS)
                y = y * g_ref[...] + b_ref[...]
                y = y * jax.nn.sigmoid(y)
            xm_scr[...] = y.astype(BF16)
            if emit_xt:
                xt_ref[...] = y.T.astype(BF16)

        xm = xm_scr[...]
        y = _dot(xm, w_ref[...])
        if has_bias:
            y = y + bias_ref[...]
        if epilogue == "glu":
            gte = _dot(xm, w2_ref[...])
            if has_bias:
                gte = gte + bias2_ref[...]
            y = y * jax.nn.sigmoid(gte)
        elif epilogue == "resid":
            y = res_ref[...] + gate_ref[...] * y
        o_ref[...] = y.astype(o_ref.dtype)

    outs = pl.pallas_call(
        body,
        grid=(n_tok // tm, nj),
        in_specs=specs,
        out_specs=out_specs,
        out_shape=out_shape,
        scratch_shapes=[pltpu.VMEM((tm, k_tot), BF16)],
        compiler_params=_params("parallel", "arbitrary"),
        name=name,
    )(*args)
    return outs if emit_xt else outs[0]


def _rope_tables(n_tokens):
    t = jnp.arange(n_tokens)
    pos = jnp.stack([(t // GRID_W).astype(F32), (t % GRID_W).astype(F32)], axis=1)
    freq = ROPE_THETA ** (-jnp.arange(ROPE_AXIS_F, dtype=F32) / ROPE_AXIS_F)
    lane = jnp.arange(LANES)
    axis = (lane % A_HD) // (2 * ROPE_AXIS_F)
    half = (lane // ROPE_AXIS_F) % 2
    ang = pos[:, axis] * freq[lane % ROPE_AXIS_F][None, :]
    cos, sin = jnp.cos(ang), jnp.sin(ang)
    sin_lo = jnp.where(half[None, :] == 1, sin, 0.0)
    sin_hi = jnp.where(half[None, :] == 0, -sin, 0.0)
    return cos, sin_lo, sin_hi


def _rope(x, cos, sin_lo, sin_hi):
    return (x * cos + pltpu.roll(x, ROPE_AXIS_F, 1) * sin_lo
            + pltpu.roll(x, LANES - ROPE_AXIS_F, 1) * sin_hi)


def _attention(proj, lamv, subln_g, lam_init, *, tq, ctx=None, rope=None):
    bsz, t_len, _ = proj.shape
    nq = t_len // tq
    has_ctx = ctx is not None
    out_scale = 1.0 - lam_init

    def body(*refs):
        it = iter(refs)
        lam_ref, g_ref, q_ref, k_ref, v_ref = (next(it) for _ in range(5))
        if has_ctx:
            ck_ref, cv_ref = next(it), next(it)
            cq_ref, slq_ref, shq_ref, ck_tab, slk_tab, shk_tab = (next(it) for _ in range(6))
        o_ref = next(it)
        if has_ctx:
            kr_scr = next(it)

            @pl.when(pl.program_id(2) == 0)
            def _():
                kr_scr[...] = _rope(k_ref[...], ck_tab[...], slk_tab[...], shk_tab[...]).astype(BF16)

        lv = lam_ref[...]
        lam = (jnp.exp(jnp.sum(lv[0:1] * lv[1:2], axis=-1, keepdims=True))
               - jnp.exp(jnp.sum(lv[2:3] * lv[3:4], axis=-1, keepdims=True)) + lam_init)
        q = q_ref[...]
        if has_ctx:
            q = _rope(q, cq_ref[...], slq_ref[...], shq_ref[...])
            k_own = kr_scr[...]
        else:
            k_own = k_ref[...].astype(BF16)
        q = q * (A_HD ** -0.5 * LOG2E)
        lane = lax.broadcasted_iota(jnp.int32, q.shape, 1)
        v_own = v_ref[...].astype(BF16)
        if has_ctx:
            k_ctx = ck_ref[...].astype(BF16)
            v_ctx = cv_ref[...].astype(BF16)

        parts = []
        for comp in range(2):
            in_comp = (lane < A_HD) if comp == 0 else (lane >= A_HD)
            qc = jnp.where(in_comp, q, 0.0).astype(BF16)
            s_own = _dot_nt(qc, k_own)
            m = jnp.max(s_own, axis=-1, keepdims=True)
            if has_ctx:
                s_ctx = _dot_nt(qc, k_ctx)
                m = jnp.maximum(m, jnp.max(s_ctx, axis=-1, keepdims=True))
            e_own = jnp.exp2(s_own - m)
            l = jnp.sum(e_own, axis=-1, keepdims=True)
            pv = _dot(e_own.astype(BF16), v_own)
            if has_ctx:
                e_ctx = jnp.exp2(s_ctx - m)
                l = l + jnp.sum(e_ctx, axis=-1, keepdims=True)
                pv = pv + _dot(e_ctx.astype(BF16), v_ctx)
            parts.append(pv * (1.0 / l))
        out = parts[0] - lam * parts[1]
        out = out * lax.rsqrt(jnp.mean(out * out, axis=-1, keepdims=True) + NORM_EPS)
        o_ref[...] = (out * g_ref[...] * out_scale).astype(o_ref.dtype)

    nh = A_HEADS
    args = [lamv, subln_g.reshape(1, 2 * A_HD), proj, proj, proj]
    specs = [
        pl.BlockSpec((8, LANES), lambda b, h, i: (0, 0)),
        pl.BlockSpec((1, 2 * A_HD), lambda b, h, i: (0, 0)),
        pl.BlockSpec((None, tq, LANES), lambda b, h, i: (b, i, h)),
        pl.BlockSpec((None, t_len, LANES), lambda b, h, i: (b, 0, nh + h)),
        pl.BlockSpec((None, t_len, LANES), lambda b, h, i: (b, 0, 2 * nh + h)),
    ]
    scratch = []
    if has_ctx:
        p_len = ctx[0].shape[1]
        args += [ctx[0], ctx[1]]
        specs += [pl.BlockSpec((None, p_len, LANES), lambda b, h, i: (b, 0, h))] * 2
        args += list(rope) + list(rope)
        specs += [pl.BlockSpec((tq, LANES), lambda b, h, i: (i, 0))] * 3
        specs += [pl.BlockSpec((t_len, LANES), lambda b, h, i: (0, 0))] * 3
        scratch = [pltpu.VMEM((t_len, LANES), BF16)]
    return pl.pallas_call(
        body,
        grid=(bsz, nh, nq),
        in_specs=specs,
        out_specs=pl.BlockSpec((None, tq, LANES), lambda b, h, i: (b, i, h)),
        out_shape=jax.ShapeDtypeStruct((bsz, t_len, A_V), BF16),
        scratch_shapes=scratch,
        compiler_params=_params("parallel", "parallel", "arbitrary"),
        name="attention_ctx" if has_ctx else "attention",
    )(*args)


def _pool(proj, pool_w, pool_scale):
    bsz, t_len, n_in = proj.shape
    col_blk = (n_in - POOL_WIDTH) // POOL_WIDTH

    def body(p_ref, w_ref, sc_ref, o_ref, pad_scr):
        zeros = jnp.zeros((HALO, POOL_C), F32)
        pad_scr[0:HALO, :] = zeros
        pad_scr[HALO + t_len:2 * HALO + t_len, :] = zeros
        tok = lax.broadcasted_iota(jnp.int32, (t_len, POOL_C), 0)
        for g, win in enumerate(POOL_WINDOWS):
            cols = slice(g * POOL_C, (g + 1) * POOL_C)
            x = p_ref[:, cols]
            pad_scr[HALO:HALO + t_len, :] = x
            back, fwd = win // 2, win - win // 2 - 1
            total = pad_scr[pl.ds(HALO - back, t_len), :]
            for o in range(-back + 1, fwd + 1):
                total = total + pad_scr[pl.ds(HALO + o, t_len), :]
            lo = jnp.maximum(tok - back, 0)
            hi = jnp.minimum(tok + fwd, t_len - 1)
            cnt = (hi - lo + 1).astype(F32)
            diff = (total / cnt - x).astype(BF16)
            o_ref[:, cols] = (_dot(diff, w_ref[g]) * sc_ref[:, cols]).astype(o_ref.dtype)

    return pl.pallas_call(
        body,
        grid=(bsz,),
        in_specs=[
            pl.BlockSpec((None, t_len, POOL_WIDTH), lambda b: (b, 0, col_blk)),
            pl.BlockSpec(pool_w.shape, lambda b: (0, 0, 0)),
            pl.BlockSpec((1, POOL_WIDTH), lambda b: (0, 0)),
        ],
        out_specs=pl.BlockSpec((None, t_len, POOL_WIDTH), lambda b: (b, 0, 0)),
        out_shape=jax.ShapeDtypeStruct((bsz, t_len, POOL_WIDTH), BF16),
        scratch_shapes=[pltpu.VMEM((t_len + 2 * HALO, POOL_C), F32)],
        compiler_params=_params("parallel"),
        name="pool",
    )(proj, pool_w, pool_scale.reshape(1, POOL_WIDTH))


def _dwconv(u, dw_w, dw_b, *, cb=256, rows=128):
    bsz, t_len, ch = u.shape
    taps = dw_w.shape[0]
    reach = taps // 2
    assert reach <= HALO and taps <= 32
    ext = t_len + 2 * HALO - 8

    def body(u_ref, w_ref, b_ref, o_ref, pad_scr, sh_scr):
        zeros = jnp.zeros((HALO, cb), F32)
        pad_scr[0:HALO, :] = zeros
        pad_scr[HALO + t_len:2 * HALO + t_len, :] = zeros
        pad_scr[HALO:HALO + t_len, :] = u_ref[...].astype(F32)
        for b in range(8):
            off = b + HALO - reach
            sh_scr[b] = pad_scr[off:off + ext, :]

        def chunk(r, carry):
            base = pl.multiple_of(r * rows, rows)
            acc = jnp.zeros((rows, cb), F32)
            for k in range(taps):
                a, b = divmod(k, 8)
                acc = acc + sh_scr[b, pl.ds(base + 8 * a, rows), :] * w_ref[k:k + 1, :]
            o_ref[pl.ds(base, rows), :] = (acc + b_ref[...]).astype(o_ref.dtype)
            return carry

        lax.fori_loop(0, t_len // rows, chunk, 0)

    w_pad = jnp.zeros((32, ch), F32).at[:taps].set(dw_w)
    return pl.pallas_call(
        body,
        grid=(bsz, ch // cb),
        in_specs=[
            pl.BlockSpec((None, t_len, cb), lambda b, c: (b, 0, c)),
            pl.BlockSpec((32, cb), lambda b, c: (0, c)),
            pl.BlockSpec((1, cb), lambda b, c: (0, c)),
        ],
        out_specs=pl.BlockSpec((None, t_len, cb), lambda b, c: (b, 0, c)),
        out_shape=jax.ShapeDtypeStruct((bsz, t_len, ch), u.dtype),
        scratch_shapes=[pltpu.VMEM((t_len + 2 * HALO, cb), F32), pltpu.VMEM((8, ext, cb), F32)],
        compiler_params=_params("parallel", "parallel"),
        name="dwconv",
    )(u, w_pad, dw_b.reshape(1, ch))


N_RANK = PEER_TOPK + 1
HEAD_UNROLL = 4
GATE_SHIFT = -1.0


def _top_values(s, count):
    vals = []
    for _ in range(count):
        m = jnp.max(s, axis=0, keepdims=True)
        vals.append(m)
        s = jnp.where(s == m, NEG_INF, s)
    return vals


def _sorting_network(n):
    pairs = []
    p = 1
    while p < n:
        k = p
        while k >= 1:
            for j in range(k % p, n - k, 2 * k):
                for i in range(min(k, n - j - k)):
                    if (i + j) // (2 * p) == (i + j + k) // (2 * p):
                        pairs.append((i + j, i + j + k))
            k //= 2
        p *= 2
    return pairs


def _top_values_sorted_lists(s, count):
    rows, cols = s.shape
    n_slab = rows // 8
    s3 = s.reshape(n_slab, 8, cols)
    v = [s3[j] for j in range(n_slab)]
    for i, j in _sorting_network(n_slab):
        hi, lo = jnp.maximum(v[i], v[j]), jnp.minimum(v[i], v[j])
        v[i], v[j] = hi, lo
    vals = []
    for a in range(count):
        m = jnp.max(v[0], axis=0, keepdims=True)
        vals.append(m)
        adv = v[0] == m
        for j in range(min(count - 1 - a, n_slab)):
            nxt = v[j + 1] if j + 1 < n_slab else NEG_INF
            v[j] = jnp.where(adv, nxt, v[j])
    return vals


def _peer_topk(q, keys, *, tt):
    n_tok = q.shape[0]
    pairs = [(a, b) for a in range(N_RANK) for b in range(N_RANK) if (a + 1) * (b + 1) <= N_RANK]

    n_cand = -(-len(pairs) // 8) * 8

    def body(q_ref, keys_ref, ab_ref, thr_ref, cand_scr):
        cand_scr[...] = jnp.full(cand_scr.shape, NEG_INF, F32)

        def head_group(hg, carry):
            for u in range(HEAD_UNROLL):
                head(hg * HEAD_UNROLL + u)
            return carry

        def head(h):
            raw, top = [], []
            for p in range(2):
                col = pl.multiple_of((2 * h + p) * PEER_NK, PEER_NK)
                s = _dot_nt(keys_ref[p].astype(BF16), q_ref[:, pl.ds(col, PEER_NK)])
                raw.append(s)
                top.append(_top_values_sorted_lists(s, N_RANK))
            for idx, (a, b) in enumerate(pairs):
                cand_scr[idx:idx + 1, :] = top[0][a] + top[1][b]
            best = _top_values(cand_scr[...], N_RANK)
            z = jnp.ones_like(best[0])
            for i in range(1, PEER_TOPK):
                z = z + jnp.exp(best[i] - best[0])
            log_z = jnp.log(z)
            mid = 0.5 * (best[PEER_TOPK - 1] + best[PEER_TOPK])
            ab_ref[2 * h] = (raw[0] - top[0][0]) * LOG2E
            ab_ref[2 * h + 1] = (raw[1] - top[1][0] - log_z) * LOG2E + GATE_SHIFT
            thr_ref[pl.ds(h, 1), :] = (mid - best[0] - log_z) * LOG2E + GATE_SHIFT

        lax.fori_loop(0, PEER_HEADS // HEAD_UNROLL, head_group, 0)

    return pl.pallas_call(
        body,
        grid=(n_tok // tt,),
        in_specs=[
            pl.BlockSpec((tt, q.shape[1]), lambda i: (i, 0)),
            pl.BlockSpec(keys.shape, lambda i: (0, 0, 0)),
        ],
        out_specs=[
            pl.BlockSpec((2 * PEER_HEADS, PEER_NK, tt), lambda i: (0, 0, i)),
            pl.BlockSpec((PEER_HEADS, tt), lambda i: (0, i)),
        ],
        out_shape=[
            jax.ShapeDtypeStruct((2 * PEER_HEADS, PEER_NK, n_tok), F32),
            jax.ShapeDtypeStruct((PEER_HEADS, n_tok), F32),
        ],
        scratch_shapes=[pltpu.VMEM((n_cand, tt), F32)],
        compiler_params=_params("parallel"),
        name="peer_topk",
    )(q, keys)


def _peer_dense(xt, u_tab, v_tab, layer, ab, thr, res, gate, *, mod_row, tt, et, lane_chunk=128):
    d, n_tok = xt.shape
    n_exp = u_tab.shape[1]
    rows_per_step = et // PEER_NK
    n_blk = n_exp // et
    n_lane = tt // lane_chunk
    assert rows_per_step == 8

    def body(xt_ref, u_ref, v_ref, ab_ref, thr_ref, res_ref, gate_ref, o_ref, acc_scr, act_scr, w_scr):
        s = pl.program_id(1)

        @pl.when(s == 0)
        def _():
            acc_scr[...] = jnp.zeros_like(acc_scr)

        grp = pl.multiple_of(s * rows_per_step, rows_per_step)
        act_scr[...] = _dot(u_ref[...], xt_ref[...])
        for c in range(n_lane):
            cs = slice(c * lane_chunk, (c + 1) * lane_chunk)
            for r in range(rows_per_step):
                rs = slice(r * PEER_NK, (r + 1) * PEER_NK)
                g = jnp.zeros((PEER_NK, lane_chunk), F32)
                for h in range(PEER_HEADS):
                    a_row = ab_ref[2 * h, pl.ds(grp, rows_per_step), cs][r:r + 1]
                    sc = a_row + ab_ref[2 * h + 1, :, cs]
                    g = g + jnp.where(sc >= thr_ref[h:h + 1, cs], jnp.exp2(sc), 0.0)
                a = act_scr[rs, cs]
                w_scr[rs, cs] = (g * (a * (1.0 + lax.erf(a * SQRT_HALF)))).astype(BF16)
        acc_scr[...] += lax.dot_general(w_scr[...], v_ref[...], (((0,), (0,)), ((), ())),
                                        preferred_element_type=F32)

        @pl.when(s == n_blk - 1)
        def _():
            o_ref[...] = res_ref[...] + gate_ref[...] * acc_scr[...]

    return pl.pallas_call(
        body,
        grid=(n_tok // tt, n_blk),
        in_specs=[
            pl.BlockSpec((d, tt), lambda i, s: (0, i)),
            pl.BlockSpec((None, et, d), lambda i, s: (layer, s, 0)),
            pl.BlockSpec((None, et, d), lambda i, s: (layer, s, 0)),
            pl.BlockSpec((2 * PEER_HEADS, PEER_NK, tt), lambda i, s: (0, 0, i)),
            pl.BlockSpec((PEER_HEADS, tt), lambda i, s: (0, i)),
            pl.BlockSpec((tt, d), lambda i, s: (i, 0)),
            pl.BlockSpec((None, 1, d), lambda i, s: (mod_row(i), 0, 0)),
        ],
        out_specs=pl.BlockSpec((tt, d), lambda i, s: (i, 0)),
        out_shape=jax.ShapeDtypeStruct((n_tok, d), F32),
        scratch_shapes=[pltpu.VMEM((tt, d), F32), pltpu.VMEM((et, tt), F32), pltpu.VMEM((et, tt), BF16)],
        compiler_params=_params("parallel", "arbitrary"),
        name="peer_dense",
    )(xt, u_tab, v_tab, ab, thr, res, gate)


def _cast_tables(tabs, *, rows=1024):
    n_l, n_e, d = tabs[0].shape

    def body(*refs):
        for src, dst in zip(refs[:len(tabs)], refs[len(tabs):]):
            dst[...] = src[...].astype(BF16)

    spec = pl.BlockSpec((None, rows, d), lambda l, i: (l, i, 0))
    return pl.pallas_call(
        body,
        grid=(n_l, n_e // rows),
        in_specs=[spec] * len(tabs),
        out_specs=[spec] * len(tabs),
        out_shape=[jax.ShapeDtypeStruct((n_l, n_e, d), BF16)] * len(tabs),
        compiler_params=_params("parallel", "parallel"),
        name="cast_tables",
    )(*tabs)


def _final_norm(x, g, *, tm):
    n_tok, d = x.shape

    def body(x_ref, g_ref, o_ref):
        xv = x_ref[...]
        o_ref[...] = xv * lax.rsqrt(jnp.mean(xv * xv, axis=-1, keepdims=True) + NORM_EPS) * g_ref[...]

    return pl.pallas_call(
        body,
        grid=(n_tok // tm,),
        in_specs=[pl.BlockSpec((tm, d), lambda i: (i, 0)), pl.BlockSpec((1, d), lambda i: (0, 0))],
        out_specs=pl.BlockSpec((tm, d), lambda i: (i, 0)),
        out_shape=jax.ShapeDtypeStruct((n_tok, d), F32),
        compiler_params=_params("parallel"),
        name="final_norm",
    )(x, g.reshape(1, d))


TM = 1024
PEER_TT = 512
PEER_ET = 1024
TOPK_TT = 256


def kernel(x_prompt, x_sample, cache_k, cache_v, c, c_ctx, norm1_g, norm2_g, final_g,
           ada_w, ada_b, w_in_e, w_out_e, lam_q1, lam_k1, lam_q2, lam_k2, subln_g,
           pool_w, pool_scale, pw1_w, pw1_b, dw_w, dw_b, cln_g, cln_b, pw2_w, pw2_b,
           peer_wq, peer_keys, peer_u, peer_v):
    d = D_MODEL
    bp, tp, _ = x_prompt.shape
    bs, ts, _ = x_sample.shape

    cvec = jnp.zeros((MOD_ROWS, d), F32).at[0].set(c_ctx).at[1:1 + bs].set(c)
    mods = _ada(cvec, ada_w, ada_b)
    mods = mods.reshape(DEPTH, MOD_ROWS, 6, 1, d).transpose(0, 2, 1, 3, 4)

    def sample_row(tile):
        return lambda i: 1 + (i * tile) // ts

    streams = [
        dict(x=x_prompt.reshape(bp * tp, d), b=bp, t=tp, mod_row=lambda tile: (lambda i: 0)),
        dict(x=x_sample.reshape(bs * ts, d), b=bs, t=ts, mod_row=sample_row),
    ]
    rope = _rope_tables(ts)
    new_k, new_v = [], []
    u_all, v_all = _cast_tables([peer_u, peer_v])

    for i in range(DEPTH):
        j = i // 2
        sh1, sc1, g1, sh2, sc2, g2 = (mods[i, m] for m in range(6))
        wq = peer_wq[i].astype(BF16)
        if i % 2 == 0:
            lam_init = 0.8 - 0.6 * math.exp(-0.3 * i)
            w_in = w_in_e[j].astype(BF16)
            w_out = w_out_e[j].astype(BF16)
            pw = pool_w[j].astype(BF16)
            lamv = jnp.zeros((8, LANES), F32)
            lamv = lamv.at[0, :A_HD].set(lam_q1[j]).at[1, :A_HD].set(lam_k1[j])
            lamv = lamv.at[2, :A_HD].set(lam_q2[j]).at[3, :A_HD].set(lam_k2[j])
        else:
            w1 = pw1_w[j].astype(BF16)
            w2 = pw2_w[j].astype(BF16)

        for si, st in enumerate(streams):
            x, mod_row = st["x"], st["mod_row"](TM)
            if i % 2 == 0:
                proj = _linear([x], w_in, mod_row=mod_row, tm=TM, tn=1024, prologue="rms_mod",
                               norm_g=norm1_g[i], shift=sh1, scale=sc1, name="w_in")
                proj3 = proj.reshape(st["b"], st["t"], IN_EVEN)
                if si == 0:
                    new_k.append(proj3[:, :, A_QK:2 * A_QK].reshape(bp, tp, A_HEADS, 2, A_HD))
                    new_v.append(proj3[:, :, 2 * A_QK:2 * A_QK + A_V].reshape(bp, tp, A_HEADS, 2 * A_HD))
                    att = _attention(proj3, lamv, subln_g[j], lam_init, tq=tp)
                else:
                    ctx = (cache_k[:, j].reshape(bs, -1, A_QK), cache_v[:, j].reshape(bs, -1, A_V))
                    att = _attention(proj3, lamv, subln_g[j], lam_init, tq=512, ctx=ctx, rope=rope)
                pooled = _pool(proj3, pw, pool_scale[j])
                x = _linear([att.reshape(-1, A_V), pooled.reshape(-1, POOL_WIDTH)], w_out,
                            mod_row=mod_row, tm=TM, tn=1024, epilogue="resid", res=x, gate=g1,
                            name="w_out")
            else:
                u = _linear([x], w1, mod_row=mod_row, tm=TM, tn=1024, prologue="rms_mod",
                            epilogue="glu", bias=pw1_b[j], norm_g=norm1_g[i], shift=sh1, scale=sc1,
                            out_dtype=BF16, name="pw1_glu")
                u = _dwconv(u.reshape(st["b"], st["t"], d), dw_w[j], dw_b[j]).reshape(-1, d)
                x = _linear([u], w2, mod_row=mod_row, tm=TM, tn=1024, prologue="ln_silu",
                            epilogue="resid", bias=pw2_b[j], norm_g=cln_g[j], norm_b=cln_b[j],
                            res=x, gate=g1, name="pw2")

            q, xt = _linear([x], wq, mod_row=mod_row, tm=TM, tn=1024, prologue="rms_mod",
                            norm_g=norm2_g[i], shift=sh2, scale=sc2, emit_xt=True, out_dtype=BF16,
                            name="peer_q")
            ab, thr = _peer_topk(q, peer_keys[i], tt=TOPK_TT)
            x = _peer_dense(xt, u_all, v_all, i, ab, thr, x, g2, mod_row=st["mod_row"](PEER_TT),
                            tt=PEER_TT, et=PEER_ET)
            st["x"] = x

    y_prompt = _final_norm(streams[0]["x"], final_g, tm=TM).reshape(bp, tp, d)
    y_sample = _final_norm(streams[1]["x"], final_g, tm=TM).reshape(bs, ts, d)
    return (y_prompt, y_sample, jnp.stack(new_k, axis=1), jnp.stack(new_v, axis=1))
```

```python
import math

import jax
import jax.numpy as jnp
from jax import lax
from jax.experimental import pallas as pl
from jax.experimental.pallas import tpu as pltpu

F32 = jnp.float32
BF16 = jnp.bfloat16

D_MODEL = 2048
DEPTH = 2
GRID_W = 64
A_HD = 64
A_HEADS = D_MODEL // (4 * A_HD)
A_QK = A_HEADS * 2 * A_HD
A_V = A_HEADS * 2 * A_HD
POOL_WIDTH = D_MODEL // 2
POOL_WINDOWS = (2, 4, 8, 16)
POOL_C = POOL_WIDTH // len(POOL_WINDOWS)
IN_EVEN = 2 * A_QK + A_V + POOL_WIDTH
CONV_K = 31
PEER_HEADS = 8
PEER_NK = 128
PEER_N = PEER_NK * PEER_NK
PEER_TOPK = 16
ROPE_THETA = 10000.0
ROPE_AXIS_F = A_HD // 4
NORM_EPS = 1e-6
LN_EPS = 1e-5

LANES = 128
MOD_ROWS = 16
VMEM_LIMIT = 56 * 1024 * 1024
HALO = 16
LOG2E = 1.4426950408889634
SQRT_HALF = 0.7071067811865476
NEG_INF = float("-inf")


def _params(*sem):
    return pltpu.CompilerParams(dimension_semantics=sem, vmem_limit_bytes=VMEM_LIMIT)


def _dot(a, b):
    return jnp.dot(a, b, preferred_element_type=F32)


def _dot_nt(a, b):
    return lax.dot_general(a, b, (((1,), (1,)), ((), ())), preferred_element_type=F32)


def _ada_body(c_ref, w_ref, b_ref, o_ref):
    c = c_ref[...]
    s = (c * jax.nn.sigmoid(c)).astype(BF16)
    o_ref[...] = _dot(s, w_ref[...].astype(BF16)) + b_ref[...]


def _ada(cvec, ada_w, ada_b):
    depth, d, n = ada_w.shape
    tn = 1024
    return pl.pallas_call(
        _ada_body,
        grid=(depth, n // tn),
        in_specs=[
            pl.BlockSpec((MOD_ROWS, d), lambda l, j: (0, 0)),
            pl.BlockSpec((None, d, tn), lambda l, j: (l, 0, j)),
            pl.BlockSpec((None, 1, tn), lambda l, j: (l, 0, j)),
        ],
        out_specs=pl.BlockSpec((None, MOD_ROWS, tn), lambda l, j: (l, 0, j)),
        out_shape=jax.ShapeDtypeStruct((depth, MOD_ROWS, n), F32),
        compiler_params=_params("parallel", "parallel"),
        name="ada",
    )(cvec, ada_w, ada_b.reshape(depth, 1, n))


def _linear(xs, w, *, mod_row, tm, tn, prologue="none", epilogue="none", bias=None,
            norm_g=None, norm_b=None, shift=None, scale=None, res=None, gate=None,
            emit_xt=False, out_dtype=F32, name="linear"):
    n_tok = xs[0].shape[0]
    k_tot = sum(x.shape[1] for x in xs)
    n_w = w.shape[1]
    n_out = n_w // 2 if epilogue == "glu" else n_w
    nj = n_out // tn
    row_map = lambda i, j: (i, 0)
    vec_map = lambda i, j: (0, 0)
    mod_map = lambda i, j: (mod_row(i), 0, 0)

    args, specs = [], []
    for x in xs:
        args.append(x)
        specs.append(pl.BlockSpec((tm, x.shape[1]), row_map))
    if prologue == "rms_mod":
        args += [norm_g.reshape(1, k_tot), shift, scale]
        specs += [pl.BlockSpec((1, k_tot), vec_map),
                  pl.BlockSpec((None, 1, k_tot), mod_map),
                  pl.BlockSpec((None, 1, k_tot), mod_map)]
    elif prologue == "ln_silu":
        args += [norm_g.reshape(1, k_tot), norm_b.reshape(1, k_tot)]
        specs += [pl.BlockSpec((1, k_tot), vec_map)] * 2
    args.append(w)
    specs.append(pl.BlockSpec((k_tot, tn), lambda i, j: (0, j)))
    if epilogue == "glu":
        args.append(w)
        specs.append(pl.BlockSpec((k_tot, tn), lambda i, j: (0, j + nj)))
    if bias is not None:
        b2 = bias.reshape(1, n_w)
        args.append(b2)
        specs.append(pl.BlockSpec((1, tn), lambda i, j: (0, j)))
        if epilogue == "glu":
            args.append(b2)
            specs.append(pl.BlockSpec((1, tn), lambda i, j: (0, j + nj)))
    if epilogue == "resid":
        args += [res, gate]
        specs += [pl.BlockSpec((tm, tn), lambda i, j: (i, j)),
                  pl.BlockSpec((None, 1, tn), lambda i, j: (mod_row(i), 0, j))]

    out_shape = [jax.ShapeDtypeStruct((n_tok, n_out), out_dtype)]
    out_specs = [pl.BlockSpec((tm, tn), lambda i, j: (i, j))]
    if emit_xt:
        out_shape.append(jax.ShapeDtypeStruct((k_tot, n_tok), BF16))
        out_specs.append(pl.BlockSpec((k_tot, tm), lambda i, j: (0, i)))
    n_x = len(xs)
    has_bias = bias is not None

    def body(*refs):
        it = iter(refs)
        x_refs = [next(it) for _ in range(n_x)]
        if prologue == "rms_mod":
            g_ref, sh_ref, sc_ref = next(it), next(it), next(it)
        elif prologue == "ln_silu":
            g_ref, b_ref = next(it), next(it)
        w_ref = next(it)
        w2_ref = next(it) if epilogue == "glu" else None
        bias_ref = next(it) if has_bias else None
        bias2_ref = next(it) if (has_bias and epilogue == "glu") else None
        if epilogue == "resid":
            res_ref, gate_ref = next(it), next(it)
        o_ref = next(it)
        xt_ref = next(it) if emit_xt else None
        xm_scr = next(it)

        @pl.when(pl.program_id(1) == 0)
        def _():
            if prologue == "none":
                off = 0
                for xr in x_refs:
                    kk = xr.shape[1]
                    xm_scr[:, off:off + kk] = xr[...].astype(BF16)
                    off += kk
                return
            x = x_refs[0][...].astype(F32)
            if prologue == "rms_mod":
                y = x * lax.rsqrt(jnp.mean(x * x, axis=-1, keepdims=True) + NORM_EPS)
                y = y * g_ref[...]
                y = y * (1.0 + sc_ref[...]) + sh_ref[...]
            else:
                mu = jnp.mean(x, axis=-1, keepdims=True)
                xc = x - mu
                y = xc * lax.rsqrt(jnp.mean(xc * xc, axis=-1, keepdims=True) + LN_EPS)
                y = y * g_ref[...] + b_ref[...]
                y = y * jax.nn.sigmoid(y)
            xm_scr[...] = y.astype(BF16)
            if emit_xt:
                xt_ref[...] = y.T.astype(BF16)

        xm = xm_scr[...]
        y = _dot(xm, w_ref[...])
        if has_bias:
            y = y + bias_ref[...]
        if epilogue == "glu":
            gte = _dot(xm, w2_ref[...])
            if has_bias:
                gte = gte + bias2_ref[...]
            y = y * jax.nn.sigmoid(gte)
        elif epilogue == "resid":
            y = res_ref[...] + gate_ref[...] * y
        o_ref[...] = y.astype(o_ref.dtype)

    outs = pl.pallas_call(
        body,
        grid=(n_tok // tm, nj),
        in_specs=specs,
        out_specs=out_specs,
        out_shape=out_shape,
        scratch_shapes=[pltpu.VMEM((tm, k_tot), BF16)],
        compiler_params=_params("parallel", "arbitrary"),
        name=name,
    )(*args)
    return outs if emit_xt else outs[0]


def _rope_tables(n_tokens):
    t = jnp.arange(n_tokens)
    pos = jnp.stack([(t // GRID_W).astype(F32), (t % GRID_W).astype(F32)], axis=1)
    freq = ROPE_THETA ** (-jnp.arange(ROPE_AXIS_F, dtype=F32) / ROPE_AXIS_F)
    lane = jnp.arange(LANES)
    axis = (lane % A_HD) // (2 * ROPE_AXIS_F)
    half = (lane // ROPE_AXIS_F) % 2
    ang = pos[:, axis] * freq[lane % ROPE_AXIS_F][None, :]
    cos, sin = jnp.cos(ang), jnp.sin(ang)
    sin_lo = jnp.where(half[None, :] == 1, sin, 0.0)
    sin_hi = jnp.where(half[None, :] == 0, -sin, 0.0)
    return cos, sin_lo, sin_hi


def _rope(x, cos, sin_lo, sin_hi):
    return (x * cos + pltpu.roll(x, ROPE_AXIS_F, 1) * sin_lo
            + pltpu.roll(x, LANES - ROPE_AXIS_F, 1) * sin_hi)


def _attention(proj, lamv, subln_g, lam_init, *, tq, ctx=None, rope=None, heads_per_step=1):
    bsz, t_len, _ = proj.shape
    nq = t_len // tq
    has_ctx = ctx is not None
    out_scale = 1.0 - lam_init
    hb = heads_per_step
    assert A_HEADS % hb == 0 and (hb == 1 or not has_ctx)

    def body(*refs):
        it = iter(refs)
        lam_ref, g_ref, q_ref, k_ref, v_ref = (next(it) for _ in range(5))
        if has_ctx:
            ck_ref, cv_ref = next(it), next(it)
            cq_ref, slq_ref, shq_ref, ck_tab, slk_tab, shk_tab = (next(it) for _ in range(6))
        o_ref = next(it)
        if has_ctx:
            kr_scr = next(it)

            @pl.when(pl.program_id(2) == 0)
            def _():
                kr_scr[...] = _rope(k_ref[...], ck_tab[...], slk_tab[...], shk_tab[...]).astype(BF16)

        lv = lam_ref[...]
        lam = (jnp.exp(jnp.sum(lv[0:1] * lv[1:2], axis=-1, keepdims=True))
               - jnp.exp(jnp.sum(lv[2:3] * lv[3:4], axis=-1, keepdims=True)) + lam_init)
        for hd in range(hb):
            cols = slice(hd * LANES, (hd + 1) * LANES)
            q = q_ref[:, cols]
            if has_ctx:
                q = _rope(q, cq_ref[...], slq_ref[...], shq_ref[...])
                k_own = kr_scr[...]
            else:
                k_own = k_ref[:, cols].astype(BF16)
            q = q * (A_HD ** -0.5 * LOG2E)
            lane = lax.broadcasted_iota(jnp.int32, q.shape, 1)
            v_own = v_ref[:, cols].astype(BF16)
            if has_ctx:
                k_ctx = ck_ref[...].astype(BF16)
                v_ctx = cv_ref[...].astype(BF16)

            parts = []
            for comp in range(2):
                in_comp = (lane < A_HD) if comp == 0 else (lane >= A_HD)
                qc = jnp.where(in_comp, q, 0.0).astype(BF16)
                s_own = _dot_nt(qc, k_own)
                m = jnp.max(s_own, axis=-1, keepdims=True)
                if has_ctx:
                    s_ctx = _dot_nt(qc, k_ctx)
                    m = jnp.maximum(m, jnp.max(s_ctx, axis=-1, keepdims=True))
                e_own = jnp.exp2(s_own - m)
                l = jnp.sum(e_own, axis=-1, keepdims=True)
                pv = _dot(e_own.astype(BF16), v_own)
                if has_ctx:
                    e_ctx = jnp.exp2(s_ctx - m)
                    l = l + jnp.sum(e_ctx, axis=-1, keepdims=True)
                    pv = pv + _dot(e_ctx.astype(BF16), v_ctx)
                parts.append(pv * (1.0 / l))
            out = parts[0] - lam * parts[1]
            out = out * lax.rsqrt(jnp.mean(out * out, axis=-1, keepdims=True) + NORM_EPS)
            o_ref[:, cols] = (out * g_ref[...] * out_scale).astype(o_ref.dtype)

    nh = A_HEADS // hb
    wid = hb * LANES
    args = [lamv, subln_g.reshape(1, 2 * A_HD), proj, proj, proj]
    specs = [
        pl.BlockSpec((8, LANES), lambda b, h, i: (0, 0)),
        pl.BlockSpec((1, 2 * A_HD), lambda b, h, i: (0, 0)),
        pl.BlockSpec((None, tq, wid), lambda b, h, i: (b, i, h)),
        pl.BlockSpec((None, t_len, wid), lambda b, h, i: (b, 0, nh + h)),
        pl.BlockSpec((None, t_len, wid), lambda b, h, i: (b, 0, 2 * nh + h)),
    ]
    scratch = []
    if has_ctx:
        p_len = ctx[0].shape[1]
        args += [ctx[0], ctx[1]]
        specs += [pl.BlockSpec((None, p_len, LANES), lambda b, h, i: (b, 0, h))] * 2
        args += list(rope) + list(rope)
        specs += [pl.BlockSpec((tq, LANES), lambda b, h, i: (i, 0))] * 3
        specs += [pl.BlockSpec((t_len, LANES), lambda b, h, i: (0, 0))] * 3
        scratch = [pltpu.VMEM((t_len, LANES), BF16)]
    return pl.pallas_call(
        body,
        grid=(bsz, nh, nq),
        in_specs=specs,
        out_specs=pl.BlockSpec((None, tq, wid), lambda b, h, i: (b, i, h)),
        out_shape=jax.ShapeDtypeStruct((bsz, t_len, A_V), BF16),
        scratch_shapes=scratch,
        compiler_params=_params("parallel", "parallel", "arbitrary"),
        name="attention_ctx" if has_ctx else "attention",
    )(*args)


def _pool(proj, pool_w, pool_scale):
    bsz, t_len, n_in = proj.shape
    col_blk = (n_in - POOL_WIDTH) // POOL_WIDTH

    def body(p_ref, w_ref, sc_ref, o_ref, pad_scr):
        zeros = jnp.zeros((HALO, POOL_C), F32)
        pad_scr[0:HALO, :] = zeros
        pad_scr[HALO + t_len:2 * HALO + t_len, :] = zeros
        tok = lax.broadcasted_iota(jnp.int32, (t_len, POOL_C), 0)
        for g, win in enumerate(POOL_WINDOWS):
            cols = slice(g * POOL_C, (g + 1) * POOL_C)
            x = p_ref[:, cols]
            pad_scr[HALO:HALO + t_len, :] = x
            back, fwd = win // 2, win - win // 2 - 1
            total = pad_scr[pl.ds(HALO - back, t_len), :]
            for o in range(-back + 1, fwd + 1):
                total = total + pad_scr[pl.ds(HALO + o, t_len), :]
            lo = jnp.maximum(tok - back, 0)
            hi = jnp.minimum(tok + fwd, t_len - 1)
            cnt = (hi - lo + 1).astype(F32)
            diff = (total / cnt - x).astype(BF16)
            o_ref[:, cols] = (_dot(diff, w_ref[g]) * sc_ref[:, cols]).astype(o_ref.dtype)

    return pl.pallas_call(
        body,
        grid=(bsz,),
        in_specs=[
            pl.BlockSpec((None, t_len, POOL_WIDTH), lambda b: (b, 0, col_blk)),
            pl.BlockSpec(pool_w.shape, lambda b: (0, 0, 0)),
            pl.BlockSpec((1, POOL_WIDTH), lambda b: (0, 0)),
        ],
        out_specs=pl.BlockSpec((None, t_len, POOL_WIDTH), lambda b: (b, 0, 0)),
        out_shape=jax.ShapeDtypeStruct((bsz, t_len, POOL_WIDTH), BF16),
        scratch_shapes=[pltpu.VMEM((t_len + 2 * HALO, POOL_C), F32)],
        compiler_params=_params("parallel"),
        name="pool",
    )(proj, pool_w, pool_scale.reshape(1, POOL_WIDTH))


def _dwconv(u, dw_w, dw_b, *, cb=256, rows=128):
    bsz, t_len, ch = u.shape
    taps = dw_w.shape[0]
    reach = taps // 2
    assert reach <= HALO and taps <= 32
    ext = t_len + 2 * HALO - 8

    def body(u_ref, w_ref, b_ref, o_ref, pad_scr, sh_scr):
        zeros = jnp.zeros((HALO, cb), F32)
        pad_scr[0:HALO, :] = zeros
        pad_scr[HALO + t_len:2 * HALO + t_len, :] = zeros
        pad_scr[HALO:HALO + t_len, :] = u_ref[...].astype(F32)
        for b in range(8):
            off = b + HALO - reach
            sh_scr[b] = pad_scr[off:off + ext, :]

        def chunk(r, carry):
            base = pl.multiple_of(r * rows, rows)
            acc = jnp.zeros((rows, cb), F32)
            for k in range(taps):
                a, b = divmod(k, 8)
                acc = acc + sh_scr[b, pl.ds(base + 8 * a, rows), :] * w_ref[k:k + 1, :]
            o_ref[pl.ds(base, rows), :] = (acc + b_ref[...]).astype(o_ref.dtype)
            return carry

        lax.fori_loop(0, t_len // rows, chunk, 0)

    w_pad = jnp.zeros((32, ch), F32).at[:taps].set(dw_w)
    return pl.pallas_call(
        body,
        grid=(bsz, ch // cb),
        in_specs=[
            pl.BlockSpec((None, t_len, cb), lambda b, c: (b, 0, c)),
            pl.BlockSpec((32, cb), lambda b, c: (0, c)),
            pl.BlockSpec((1, cb), lambda b, c: (0, c)),
        ],
        out_specs=pl.BlockSpec((None, t_len, cb), lambda b, c: (b, 0, c)),
        out_shape=jax.ShapeDtypeStruct((bsz, t_len, ch), u.dtype),
        scratch_shapes=[pltpu.VMEM((t_len + 2 * HALO, cb), F32), pltpu.VMEM((8, ext, cb), F32)],
        compiler_params=_params("parallel", "parallel"),
        name="dwconv",
    )(u, w_pad, dw_b.reshape(1, ch))


N_RANK = PEER_TOPK + 1
HEAD_UNROLL = 4
GATE_SHIFT = -1.0


def _top_values(s, count):
    vals = []
    for _ in range(count):
        m = jnp.max(s, axis=0, keepdims=True)
        vals.append(m)
        s = jnp.where(s == m, NEG_INF, s)
    return vals


def _sorting_network(n):
    pairs = []
    p = 1
    while p < n:
        k = p
        while k >= 1:
            for j in range(k % p, n - k, 2 * k):
                for i in range(min(k, n - j - k)):
                    if (i + j) // (2 * p) == (i + j + k) // (2 * p):
                        pairs.append((i + j, i + j + k))
            k //= 2
        p *= 2
    return pairs


def _top_values_sorted_lists(s, count):
    rows, cols = s.shape
    n_slab = rows // 8
    s3 = s.reshape(n_slab, 8, cols)
    v = [s3[j] for j in range(n_slab)]
    for i, j in _sorting_network(n_slab):
        hi, lo = jnp.maximum(v[i], v[j]), jnp.minimum(v[i], v[j])
        v[i], v[j] = hi, lo
    vals = []
    for a in range(count):
        m = jnp.max(v[0], axis=0, keepdims=True)
        vals.append(m)
        adv = v[0] == m
        for j in range(min(count - 1 - a, n_slab)):
            nxt = v[j + 1] if j + 1 < n_slab else NEG_INF
            v[j] = jnp.where(adv, nxt, v[j])
    return vals


def _peer_topk(q, keys, *, tt):
    n_tok = q.shape[0]
    pairs = [(a, b) for a in range(N_RANK) for b in range(N_RANK) if (a + 1) * (b + 1) <= N_RANK]

    n_cand = -(-len(pairs) // 8) * 8

    def body(q_ref, keys_ref, ab_ref, thr_ref, cand_scr):
        cand_scr[...] = jnp.full(cand_scr.shape, NEG_INF, F32)

        def head_group(hg, carry):
            for u in range(HEAD_UNROLL):
                head(hg * HEAD_UNROLL + u)
            return carry

        def head(h):
            raw, top = [], []
            for p in range(2):
                col = pl.multiple_of((2 * h + p) * PEER_NK, PEER_NK)
                s = _dot_nt(keys_ref[p].astype(BF16), q_ref[:, pl.ds(col, PEER_NK)])
                raw.append(s)
                top.append(_top_values_sorted_lists(s, N_RANK))
            for idx, (a, b) in enumerate(pairs):
                cand_scr[idx:idx + 1, :] = top[0][a] + top[1][b]
            best = _top_values(cand_scr[...], N_RANK)
            z = jnp.ones_like(best[0])
            for i in range(1, PEER_TOPK):
                z = z + jnp.exp(best[i] - best[0])
            log_z = jnp.log(z)
            mid = 0.5 * (best[PEER_TOPK - 1] + best[PEER_TOPK])
            ab_ref[2 * h] = (raw[0] - top[0][0]) * LOG2E
            ab_ref[2 * h + 1] = (raw[1] - top[1][0] - log_z) * LOG2E + GATE_SHIFT
            thr_ref[pl.ds(h, 1), :] = (mid - best[0] - log_z) * LOG2E + GATE_SHIFT

        lax.fori_loop(0, PEER_HEADS // HEAD_UNROLL, head_group, 0)

    return pl.pallas_call(
        body,
        grid=(n_tok // tt,),
        in_specs=[
            pl.BlockSpec((tt, q.shape[1]), lambda i: (i, 0)),
            pl.BlockSpec(keys.shape, lambda i: (0, 0, 0)),
        ],
        out_specs=[
            pl.BlockSpec((2 * PEER_HEADS, PEER_NK, tt), lambda i: (0, 0, i)),
            pl.BlockSpec((PEER_HEADS, tt), lambda i: (0, i)),
        ],
        out_shape=[
            jax.ShapeDtypeStruct((2 * PEER_HEADS, PEER_NK, n_tok), F32),
            jax.ShapeDtypeStruct((PEER_HEADS, n_tok), F32),
        ],
        scratch_shapes=[pltpu.VMEM((n_cand, tt), F32)],
        compiler_params=_params("parallel"),
        name="peer_topk",
    )(q, keys)


def _peer_dense(xt, u_tab, v_tab, layer, ab, thr, res, gate, *, mod_row, tt, et, lane_chunk=128):
    d, n_tok = xt.shape
    n_exp = u_tab.shape[1]
    rows_per_step = et // PEER_NK
    n_blk = n_exp // et
    n_lane = tt // lane_chunk
    assert rows_per_step == 8

    def body(xt_ref, u_ref, v_ref, ab_ref, thr_ref, res_ref, gate_ref, o_ref, acc_scr, act_scr, w_scr):
        s = pl.program_id(1)

        @pl.when(s == 0)
        def _():
            acc_scr[...] = jnp.zeros_like(acc_scr)

        grp = pl.multiple_of(s * rows_per_step, rows_per_step)
        act_scr[...] = _dot(u_ref[...], xt_ref[...])
        for c in range(n_lane):
            cs = slice(c * lane_chunk, (c + 1) * lane_chunk)
            for r in range(rows_per_step):
                rs = slice(r * PEER_NK, (r + 1) * PEER_NK)
                g = jnp.zeros((PEER_NK, lane_chunk), F32)
                for h in range(PEER_HEADS):
                    a_row = ab_ref[2 * h, pl.ds(grp, rows_per_step), cs][r:r + 1]
                    sc = a_row + ab_ref[2 * h + 1, :, cs]
                    g = g + jnp.where(sc >= thr_ref[h:h + 1, cs], jnp.exp2(sc), 0.0)
                a = act_scr[rs, cs]
                w_scr[rs, cs] = (g * (a * (1.0 + lax.erf(a * SQRT_HALF)))).astype(BF16)
        acc_scr[...] += lax.dot_general(w_scr[...], v_ref[...], (((0,), (0,)), ((), ())),
                                        preferred_element_type=F32)

        @pl.when(s == n_blk - 1)
        def _():
            o_ref[...] = res_ref[...] + gate_ref[...] * acc_scr[...]

    return pl.pallas_call(
        body,
        grid=(n_tok // tt, n_blk),
        in_specs=[
            pl.BlockSpec((d, tt), lambda i, s: (0, i)),
            pl.BlockSpec((None, et, d), lambda i, s: (layer, s, 0)),
            pl.BlockSpec((None, et, d), lambda i, s: (layer, s, 0)),
            pl.BlockSpec((2 * PEER_HEADS, PEER_NK, tt), lambda i, s: (0, 0, i)),
            pl.BlockSpec((PEER_HEADS, tt), lambda i, s: (0, i)),
            pl.BlockSpec((tt, d), lambda i, s: (i, 0)),
            pl.BlockSpec((None, 1, d), lambda i, s: (mod_row(i), 0, 0)),
        ],
        out_specs=pl.BlockSpec((tt, d), lambda i, s: (i, 0)),
        out_shape=jax.ShapeDtypeStruct((n_tok, d), F32),
        scratch_shapes=[pltpu.VMEM((tt, d), F32), pltpu.VMEM((et, tt), F32), pltpu.VMEM((et, tt), BF16)],
        compiler_params=_params("parallel", "arbitrary"),
        name="peer_dense",
    )(xt, u_tab, v_tab, ab, thr, res, gate)


def _cast_tables(tabs, *, rows=1024):
    n_l, n_e, d = tabs[0].shape

    def body(*refs):
        for src, dst in zip(refs[:len(tabs)], refs[len(tabs):]):
            dst[...] = src[...].astype(BF16)

    spec = pl.BlockSpec((None, rows, d), lambda l, i: (l, i, 0))
    return pl.pallas_call(
        body,
        grid=(n_l, n_e // rows),
        in_specs=[spec] * len(tabs),
        out_specs=[spec] * len(tabs),
        out_shape=[jax.ShapeDtypeStruct((n_l, n_e, d), BF16)] * len(tabs),
        compiler_params=_params("parallel", "parallel"),
        name="cast_tables",
    )(*tabs)


def _final_norm(x, g, *, tm):
    n_tok, d = x.shape

    def body(x_ref, g_ref, o_ref):
        xv = x_ref[...]
        o_ref[...] = xv * lax.rsqrt(jnp.mean(xv * xv, axis=-1, keepdims=True) + NORM_EPS) * g_ref[...]

    return pl.pallas_call(
        body,
        grid=(n_tok // tm,),
        in_specs=[pl.BlockSpec((tm, d), lambda i: (i, 0)), pl.BlockSpec((1, d), lambda i: (0, 0))],
        out_specs=pl.BlockSpec((tm, d), lambda i: (i, 0)),
        out_shape=jax.ShapeDtypeStruct((n_tok, d), F32),
        compiler_params=_params("parallel"),
        name="final_norm",
    )(x, g.reshape(1, d))


TM = 1024
PEER_TT = 512
PEER_ET = 1024
TOPK_TT = 256


def kernel(x_prompt, x_sample, cache_k, cache_v, c, c_ctx, norm1_g, norm2_g, final_g,
           ada_w, ada_b, w_in_e, w_out_e, lam_q1, lam_k1, lam_q2, lam_k2, subln_g,
           pool_w, pool_scale, pw1_w, pw1_b, dw_w, dw_b, cln_g, cln_b, pw2_w, pw2_b,
           peer_wq, peer_keys, peer_u, peer_v):
    d = D_MODEL
    bp, tp, _ = x_prompt.shape
    bs, ts, _ = x_sample.shape

    cvec = jnp.zeros((MOD_ROWS, d), F32).at[0].set(c_ctx).at[1:1 + bs].set(c)
    mods = _ada(cvec, ada_w, ada_b)
    mods = mods.reshape(DEPTH, MOD_ROWS, 6, 1, d).transpose(0, 2, 1, 3, 4)

    def sample_row(tile):
        return lambda i: 1 + (i * tile) // ts

    streams = [
        dict(x=x_prompt.reshape(bp * tp, d), b=bp, t=tp, mod_row=lambda tile: (lambda i: 0)),
        dict(x=x_sample.reshape(bs * ts, d), b=bs, t=ts, mod_row=sample_row),
    ]
    rope = _rope_tables(ts)
    new_k, new_v = [], []
    u_all, v_all = _cast_tables([peer_u, peer_v])

    for i in range(DEPTH):
        j = i // 2
        sh1, sc1, g1, sh2, sc2, g2 = (mods[i, m] for m in range(6))
        wq = peer_wq[i].astype(BF16)
        if i % 2 == 0:
            lam_init = 0.8 - 0.6 * math.exp(-0.3 * i)
            w_in = w_in_e[j].astype(BF16)
            w_out = w_out_e[j].astype(BF16)
            pw = pool_w[j].astype(BF16)
            lamv = jnp.zeros((8, LANES), F32)
            lamv = lamv.at[0, :A_HD].set(lam_q1[j]).at[1, :A_HD].set(lam_k1[j])
            lamv = lamv.at[2, :A_HD].set(lam_q2[j]).at[3, :A_HD].set(lam_k2[j])
        else:
            w1 = pw1_w[j].astype(BF16)
            w2 = pw2_w[j].astype(BF16)

        for si, st in enumerate(streams):
            x, mod_row = st["x"], st["mod_row"](TM)
            if i % 2 == 0:
                proj = _linear([x], w_in, mod_row=mod_row, tm=TM, tn=1024, prologue="rms_mod",
                               norm_g=norm1_g[i], shift=sh1, scale=sc1, name="w_in")
                proj3 = proj.reshape(st["b"], st["t"], IN_EVEN)
                if si == 0:
                    new_k.append(proj3[:, :, A_QK:2 * A_QK].reshape(bp, tp, A_HEADS, 2, A_HD))
                    new_v.append(proj3[:, :, 2 * A_QK:2 * A_QK + A_V].reshape(bp, tp, A_HEADS, 2 * A_HD))
                    att = _attention(proj3, lamv, subln_g[j], lam_init, tq=tp, heads_per_step=4)
                else:
                    ctx = (cache_k[:, j].reshape(bs, -1, A_QK), cache_v[:, j].reshape(bs, -1, A_V))
                    att = _attention(proj3, lamv, subln_g[j], lam_init, tq=1024, ctx=ctx, rope=rope)
                pooled = _pool(proj3, pw, pool_scale[j])
                x = _linear([att.reshape(-1, A_V), pooled.reshape(-1, POOL_WIDTH)], w_out,
                            mod_row=mod_row, tm=TM, tn=1024, epilogue="resid", res=x, gate=g1,
                            name="w_out")
            else:
                u = _linear([x], w1, mod_row=mod_row, tm=TM, tn=1024, prologue="rms_mod",
                            epilogue="glu", bias=pw1_b[j], norm_g=norm1_g[i], shift=sh1, scale=sc1,
                            out_dtype=BF16, name="pw1_glu")
                u = _dwconv(u.reshape(st["b"], st["t"], d), dw_w[j], dw_b[j]).reshape(-1, d)
                x = _linear([u], w2, mod_row=mod_row, tm=TM, tn=1024, prologue="ln_silu",
                            epilogue="resid", bias=pw2_b[j], norm_g=cln_g[j], norm_b=cln_b[j],
                            res=x, gate=g1, name="pw2")

            q, xt = _linear([x], wq, mod_row=mod_row, tm=TM, tn=1024, prologue="rms_mod",
                            norm_g=norm2_g[i], shift=sh2, scale=sc2, emit_xt=True, out_dtype=BF16,
                            name="peer_q")
            ab, thr = _peer_topk(q, peer_keys[i], tt=TOPK_TT)
            x = _peer_dense(xt, u_all, v_all, i, ab, thr, x, g2, mod_row=st["mod_row"](PEER_TT),
                            tt=PEER_TT, et=PEER_ET)
            st["x"] = x

    y_prompt = _final_norm(streams[0]["x"], final_g, tm=TM).reshape(bp, tp, d)
    y_sample = _final_norm(streams[1]["x"], final_g, tm=TM).reshape(bs, ts, d)
    return (y_prompt, y_sample, jnp.stack(new_k, axis=1), jnp.stack(new_v, axis=1))
```

```python
import math

import jax
import jax.numpy as jnp
from jax import lax
from jax.experimental import pallas as pl
from jax.experimental.pallas import tpu as pltpu

F32 = jnp.float32
BF16 = jnp.bfloat16

D_MODEL = 2048
DEPTH = 2
GRID_W = 64
A_HD = 64
A_HEADS = D_MODEL // (4 * A_HD)
A_QK = A_HEADS * 2 * A_HD
A_V = A_HEADS * 2 * A_HD
POOL_WIDTH = D_MODEL // 2
POOL_WINDOWS = (2, 4, 8, 16)
POOL_C = POOL_WIDTH // len(POOL_WINDOWS)
IN_EVEN = 2 * A_QK + A_V + POOL_WIDTH
CONV_K = 31
PEER_HEADS = 8
PEER_NK = 128
PEER_N = PEER_NK * PEER_NK
PEER_TOPK = 16
ROPE_THETA = 10000.0
ROPE_AXIS_F = A_HD // 4
NORM_EPS = 1e-6
LN_EPS = 1e-5

LANES = 128
MOD_ROWS = 16
VMEM_LIMIT = 56 * 1024 * 1024
HALO = 16
LOG2E = 1.4426950408889634
SQRT_HALF = 0.7071067811865476
NEG_INF = float("-inf")


def _params(*sem):
    return pltpu.CompilerParams(dimension_semantics=sem, vmem_limit_bytes=VMEM_LIMIT)


def _dot(a, b):
    return jnp.dot(a, b, preferred_element_type=F32)


def _dot_nt(a, b):
    return lax.dot_general(a, b, (((1,), (1,)), ((), ())), preferred_element_type=F32)


def _ada_body(c_ref, w_ref, b_ref, o_ref):
    c = c_ref[...]
    s = (c * jax.nn.sigmoid(c)).astype(BF16)
    o_ref[...] = _dot(s, w_ref[...].astype(BF16)) + b_ref[...]


def _ada(cvec, ada_w, ada_b):
    depth, d, n = ada_w.shape
    tn = 1024
    return pl.pallas_call(
        _ada_body,
        grid=(depth, n // tn),
        in_specs=[
            pl.BlockSpec((MOD_ROWS, d), lambda l, j: (0, 0)),
            pl.BlockSpec((None, d, tn), lambda l, j: (l, 0, j)),
            pl.BlockSpec((None, 1, tn), lambda l, j: (l, 0, j)),
        ],
        out_specs=pl.BlockSpec((None, MOD_ROWS, tn), lambda l, j: (l, 0, j)),
        out_shape=jax.ShapeDtypeStruct((depth, MOD_ROWS, n), F32),
        compiler_params=_params("parallel", "parallel"),
        name="ada",
    )(cvec, ada_w, ada_b.reshape(depth, 1, n))


def _linear(xs, w, *, mod_row, tm, tn, prologue="none", epilogue="none", bias=None,
            norm_g=None, norm_b=None, shift=None, scale=None, res=None, gate=None,
            emit_xt=False, out_dtype=F32, name="linear"):
    n_tok = xs[0].shape[0]
    k_tot = sum(x.shape[1] for x in xs)
    n_w = w.shape[1]
    n_out = n_w // 2 if epilogue == "glu" else n_w
    nj = n_out // tn
    row_map = lambda i, j: (i, 0)
    vec_map = lambda i, j: (0, 0)
    mod_map = lambda i, j: (mod_row(i), 0, 0)

    args, specs = [], []
    for x in xs:
        args.append(x)
        specs.append(pl.BlockSpec((tm, x.shape[1]), row_map))
    if prologue == "rms_mod":
        args += [norm_g.reshape(1, k_tot), shift, scale]
        specs += [pl.BlockSpec((1, k_tot), vec_map),
                  pl.BlockSpec((None, 1, k_tot), mod_map),
                  pl.BlockSpec((None, 1, k_tot), mod_map)]
    elif prologue == "ln_silu":
        args += [norm_g.reshape(1, k_tot), norm_b.reshape(1, k_tot)]
        specs += [pl.BlockSpec((1, k_tot), vec_map)] * 2
    args.append(w)
    specs.append(pl.BlockSpec((k_tot, tn), lambda i, j: (0, j)))
    if epilogue == "glu":
        args.append(w)
        specs.append(pl.BlockSpec((k_tot, tn), lambda i, j: (0, j + nj)))
    if bias is not None:
        b2 = bias.reshape(1, n_w)
        args.append(b2)
        specs.append(pl.BlockSpec((1, tn), lambda i, j: (0, j)))
        if epilogue == "glu":
            args.append(b2)
            specs.append(pl.BlockSpec((1, tn), lambda i, j: (0, j + nj)))
    if epilogue == "resid":
        args += [res, gate]
        specs += [pl.BlockSpec((tm, tn), lambda i, j: (i, j)),
                  pl.BlockSpec((None, 1, tn), lambda i, j: (mod_row(i), 0, j))]

    out_shape = [jax.ShapeDtypeStruct((n_tok, n_out), out_dtype)]
    out_specs = [pl.BlockSpec((tm, tn), lambda i, j: (i, j))]
    if emit_xt:
        out_shape.append(jax.ShapeDtypeStruct((k_tot, n_tok), BF16))
        out_specs.append(pl.BlockSpec((k_tot, tm), lambda i, j: (0, i)))
    n_x = len(xs)
    has_bias = bias is not None

    def body(*refs):
        it = iter(refs)
        x_refs = [next(it) for _ in range(n_x)]
        if prologue == "rms_mod":
            g_ref, sh_ref, sc_ref = next(it), next(it), next(it)
        elif prologue == "ln_silu":
            g_ref, b_ref = next(it), next(it)
        w_ref = next(it)
        w2_ref = next(it) if epilogue == "glu" else None
        bias_ref = next(it) if has_bias else None
        bias2_ref = next(it) if (has_bias and epilogue == "glu") else None
        if epilogue == "resid":
            res_ref, gate_ref = next(it), next(it)
        o_ref = next(it)
        xt_ref = next(it) if emit_xt else None
        xm_scr = next(it)

        @pl.when(pl.program_id(1) == 0)
        def _():
            if prologue == "none":
                off = 0
                for xr in x_refs:
                    kk = xr.shape[1]
                    xm_scr[:, off:off + kk] = xr[...].astype(BF16)
                    off += kk
                return
            x = x_refs[0][...].astype(F32)
            if prologue == "rms_mod":
                y = x * lax.rsqrt(jnp.mean(x * x, axis=-1, keepdims=True) + NORM_EPS)
                y = y * g_ref[...]
                y = y * (1.0 + sc_ref[...]) + sh_ref[...]
            else:
                mu = jnp.mean(x, axis=-1, keepdims=True)
                xc = x - mu
                y = xc * lax.rsqrt(jnp.mean(xc * xc, axis=-1, keepdims=True) + LN_EPS)
                y = y * g_ref[...] + b_ref[...]
                y = y * jax.nn.sigmoid(y)
            xm_scr[...] = y.astype(BF16)
            if emit_xt:
                xt_ref[...] = y.T.astype(BF16)

        xm = xm_scr[...]
        y = _dot(xm, w_ref[...])
        if has_bias:
            y = y + bias_ref[...]
        if epilogue == "glu":
            gte = _dot(xm, w2_ref[...])
            if has_bias:
                gte = gte + bias2_ref[...]
            y = y * jax.nn.sigmoid(gte)
        elif epilogue == "resid":
            y = res_ref[...] + gate_ref[...] * y
        o_ref[...] = y.astype(o_ref.dtype)

    outs = pl.pallas_call(
        body,
        grid=(n_tok // tm, nj),
        in_specs=specs,
        out_specs=out_specs,
        out_shape=out_shape,
        scratch_shapes=[pltpu.VMEM((tm, k_tot), BF16)],
        compiler_params=_params("parallel", "arbitrary"),
        name=name,
    )(*args)
    return outs if emit_xt else outs[0]


def _rope_tables(n_tokens):
    t = jnp.arange(n_tokens)
    pos = jnp.stack([(t // GRID_W).astype(F32), (t % GRID_W).astype(F32)], axis=1)
    freq = ROPE_THETA ** (-jnp.arange(ROPE_AXIS_F, dtype=F32) / ROPE_AXIS_F)
    lane = jnp.arange(LANES)
    axis = (lane % A_HD) // (2 * ROPE_AXIS_F)
    half = (lane // ROPE_AXIS_F) % 2
    ang = pos[:, axis] * freq[lane % ROPE_AXIS_F][None, :]
    cos, sin = jnp.cos(ang), jnp.sin(ang)
    sin_lo = jnp.where(half[None, :] == 1, sin, 0.0)
    sin_hi = jnp.where(half[None, :] == 0, -sin, 0.0)
    return cos, sin_lo, sin_hi


def _rope(x, cos, sin_lo, sin_hi):
    return (x * cos + pltpu.roll(x, ROPE_AXIS_F, 1) * sin_lo
            + pltpu.roll(x, LANES - ROPE_AXIS_F, 1) * sin_hi)


def _attention(proj, lamv, subln_g, lam_init, *, tq, ctx=None, rope=None, heads_per_step=1):
    bsz, t_len, _ = proj.shape
    nq = t_len // tq
    has_ctx = ctx is not None
    out_scale = 1.0 - lam_init
    hb = heads_per_step
    assert A_HEADS % hb == 0 and (hb == 1 or not has_ctx)

    def body(*refs):
        it = iter(refs)
        lam_ref, g_ref, q_ref, k_ref, v_ref = (next(it) for _ in range(5))
        if has_ctx:
            ck_ref, cv_ref = next(it), next(it)
            cq_ref, slq_ref, shq_ref, ck_tab, slk_tab, shk_tab = (next(it) for _ in range(6))
        o_ref = next(it)
        if has_ctx:
            kr_scr = next(it)

            @pl.when(pl.program_id(2) == 0)
            def _():
                kr_scr[...] = _rope(k_ref[...], ck_tab[...], slk_tab[...], shk_tab[...]).astype(BF16)

        lv = lam_ref[...]
        lam = (jnp.exp(jnp.sum(lv[0:1] * lv[1:2], axis=-1, keepdims=True))
               - jnp.exp(jnp.sum(lv[2:3] * lv[3:4], axis=-1, keepdims=True)) + lam_init)
        for hd in range(hb):
            cols = slice(hd * LANES, (hd + 1) * LANES)
            q = q_ref[:, cols]
            if has_ctx:
                q = _rope(q, cq_ref[...], slq_ref[...], shq_ref[...])
                k_own = kr_scr[...]
            else:
                k_own = k_ref[:, cols].astype(BF16)
            q = q * (A_HD ** -0.5 * LOG2E)
            lane = lax.broadcasted_iota(jnp.int32, q.shape, 1)
            v_own = v_ref[:, cols].astype(BF16)
            if has_ctx:
                k_ctx = ck_ref[...].astype(BF16)
                v_ctx = cv_ref[...].astype(BF16)

            parts = []
            for comp in range(2):
                in_comp = (lane < A_HD) if comp == 0 else (lane >= A_HD)
                qc = jnp.where(in_comp, q, 0.0).astype(BF16)
                s_own = _dot_nt(qc, k_own)
                m = jnp.max(s_own, axis=-1, keepdims=True)
                if has_ctx:
                    s_ctx = _dot_nt(qc, k_ctx)
                    m = jnp.maximum(m, jnp.max(s_ctx, axis=-1, keepdims=True))
                e_own = jnp.exp2(s_own - m)
                l = jnp.sum(e_own, axis=-1, keepdims=True)
                pv = _dot(e_own.astype(BF16), v_own)
                if has_ctx:
                    e_ctx = jnp.exp2(s_ctx - m)
                    l = l + jnp.sum(e_ctx, axis=-1, keepdims=True)
                    pv = pv + _dot(e_ctx.astype(BF16), v_ctx)
                parts.append(pv * (1.0 / l))
            out = parts[0] - lam * parts[1]
            out = out * lax.rsqrt(jnp.mean(out * out, axis=-1, keepdims=True) + NORM_EPS)
            o_ref[:, cols] = (out * g_ref[...] * out_scale).astype(o_ref.dtype)

    nh = A_HEADS // hb
    wid = hb * LANES
    args = [lamv, subln_g.reshape(1, 2 * A_HD), proj, proj, proj]
    specs = [
        pl.BlockSpec((8, LANES), lambda b, h, i: (0, 0)),
        pl.BlockSpec((1, 2 * A_HD), lambda b, h, i: (0, 0)),
        pl.BlockSpec((None, tq, wid), lambda b, h, i: (b, i, h)),
        pl.BlockSpec((None, t_len, wid), lambda b, h, i: (b, 0, nh + h)),
        pl.BlockSpec((None, t_len, wid), lambda b, h, i: (b, 0, 2 * nh + h)),
    ]
    scratch = []
    if has_ctx:
        p_len = ctx[0].shape[1]
        args += [ctx[0], ctx[1]]
        specs += [pl.BlockSpec((None, p_len, LANES), lambda b, h, i: (b, 0, h))] * 2
        args += list(rope) + list(rope)
        specs += [pl.BlockSpec((tq, LANES), lambda b, h, i: (i, 0))] * 3
        specs += [pl.BlockSpec((t_len, LANES), lambda b, h, i: (0, 0))] * 3
        scratch = [pltpu.VMEM((t_len, LANES), BF16)]
    return pl.pallas_call(
        body,
        grid=(bsz, nh, nq),
        in_specs=specs,
        out_specs=pl.BlockSpec((None, tq, wid), lambda b, h, i: (b, i, h)),
        out_shape=jax.ShapeDtypeStruct((bsz, t_len, A_V), BF16),
        scratch_shapes=scratch,
        compiler_params=_params("parallel", "parallel", "arbitrary"),
        name="attention_ctx" if has_ctx else "attention",
    )(*args)


def _pool(proj, pool_w, pool_scale):
    bsz, t_len, n_in = proj.shape
    col_blk = (n_in - POOL_WIDTH) // POOL_WIDTH

    def body(p_ref, w_ref, sc_ref, o_ref, pad_scr):
        zeros = jnp.zeros((HALO, POOL_C), F32)
        pad_scr[0:HALO, :] = zeros
        pad_scr[HALO + t_len:2 * HALO + t_len, :] = zeros
        tok = lax.broadcasted_iota(jnp.int32, (t_len, POOL_C), 0)
        for g, win in enumerate(POOL_WINDOWS):
            cols = slice(g * POOL_C, (g + 1) * POOL_C)
            x = p_ref[:, cols]
            pad_scr[HALO:HALO + t_len, :] = x
            back, fwd = win // 2, win - win // 2 - 1
            total = pad_scr[pl.ds(HALO - back, t_len), :]
            for o in range(-back + 1, fwd + 1):
                total = total + pad_scr[pl.ds(HALO + o, t_len), :]
            lo = jnp.maximum(tok - back, 0)
            hi = jnp.minimum(tok + fwd, t_len - 1)
            cnt = (hi - lo + 1).astype(F32)
            diff = (total / cnt - x).astype(BF16)
            o_ref[:, cols] = (_dot(diff, w_ref[g]) * sc_ref[:, cols]).astype(o_ref.dtype)

    return pl.pallas_call(
        body,
        grid=(bsz,),
        in_specs=[
            pl.BlockSpec((None, t_len, POOL_WIDTH), lambda b: (b, 0, col_blk)),
            pl.BlockSpec(pool_w.shape, lambda b: (0, 0, 0)),
            pl.BlockSpec((1, POOL_WIDTH), lambda b: (0, 0)),
        ],
        out_specs=pl.BlockSpec((None, t_len, POOL_WIDTH), lambda b: (b, 0, 0)),
        out_shape=jax.ShapeDtypeStruct((bsz, t_len, POOL_WIDTH), BF16),
        scratch_shapes=[pltpu.VMEM((t_len + 2 * HALO, POOL_C), F32)],
        compiler_params=_params("parallel"),
        name="pool",
    )(proj, pool_w, pool_scale.reshape(1, POOL_WIDTH))


def _dwconv(u, dw_w, dw_b, *, cb=256, rows=128):
    bsz, t_len, ch = u.shape
    taps = dw_w.shape[0]
    reach = taps // 2
    assert reach <= HALO and taps <= 32
    ext = t_len + 2 * HALO - 8

    def body(u_ref, w_ref, b_ref, o_ref, pad_scr, sh_scr):
        zeros = jnp.zeros((HALO, cb), F32)
        pad_scr[0:HALO, :] = zeros
        pad_scr[HALO + t_len:2 * HALO + t_len, :] = zeros
        pad_scr[HALO:HALO + t_len, :] = u_ref[...].astype(F32)
        for b in range(8):
            off = b + HALO - reach
            sh_scr[b] = pad_scr[off:off + ext, :]

        def chunk(r, carry):
            base = pl.multiple_of(r * rows, rows)
            acc = jnp.zeros((rows, cb), F32)
            for k in range(taps):
                a, b = divmod(k, 8)
                acc = acc + sh_scr[b, pl.ds(base + 8 * a, rows), :] * w_ref[k:k + 1, :]
            o_ref[pl.ds(base, rows), :] = (acc + b_ref[...]).astype(o_ref.dtype)
            return carry

        lax.fori_loop(0, t_len // rows, chunk, 0)

    w_pad = jnp.zeros((32, ch), F32).at[:taps].set(dw_w)
    return pl.pallas_call(
        body,
        grid=(bsz, ch // cb),
        in_specs=[
            pl.BlockSpec((None, t_len, cb), lambda b, c: (b, 0, c)),
            pl.BlockSpec((32, cb), lambda b, c: (0, c)),
            pl.BlockSpec((1, cb), lambda b, c: (0, c)),
        ],
        out_specs=pl.BlockSpec((None, t_len, cb), lambda b, c: (b, 0, c)),
        out_shape=jax.ShapeDtypeStruct((bsz, t_len, ch), u.dtype),
        scratch_shapes=[pltpu.VMEM((t_len + 2 * HALO, cb), F32), pltpu.VMEM((8, ext, cb), F32)],
        compiler_params=_params("parallel", "parallel"),
        name="dwconv",
    )(u, w_pad, dw_b.reshape(1, ch))


N_RANK = PEER_TOPK + 1
HEAD_UNROLL = 4
GATE_SHIFT = -1.0


def _top_values(s, count):
    vals = []
    for _ in range(count):
        m = jnp.max(s, axis=0, keepdims=True)
        vals.append(m)
        s = jnp.where(s == m, NEG_INF, s)
    return vals


def _sorting_network(n):
    pairs = []
    p = 1
    while p < n:
        k = p
        while k >= 1:
            for j in range(k % p, n - k, 2 * k):
                for i in range(min(k, n - j - k)):
                    if (i + j) // (2 * p) == (i + j + k) // (2 * p):
                        pairs.append((i + j, i + j + k))
            k //= 2
        p *= 2
    return pairs


def _top_values_sorted_lists(s, count):
    rows, cols = s.shape
    n_slab = rows // 8
    s3 = s.reshape(n_slab, 8, cols)
    v = [s3[j] for j in range(n_slab)]
    for i, j in _sorting_network(n_slab):
        hi, lo = jnp.maximum(v[i], v[j]), jnp.minimum(v[i], v[j])
        v[i], v[j] = hi, lo
    vals = []
    for a in range(count):
        m = jnp.max(v[0], axis=0, keepdims=True)
        vals.append(m)
        adv = v[0] == m
        for j in range(min(count - 1 - a, n_slab)):
            nxt = v[j + 1] if j + 1 < n_slab else NEG_INF
            v[j] = jnp.where(adv, nxt, v[j])
    return vals


def _peer_topk(q, keys, *, tt):
    n_tok = q.shape[0]
    pairs = [(a, b) for a in range(N_RANK) for b in range(N_RANK) if (a + 1) * (b + 1) <= N_RANK]

    n_cand = -(-len(pairs) // 8) * 8

    def body(q_ref, keys_ref, ab_ref, thr_ref, cand_scr):
        cand_scr[...] = jnp.full(cand_scr.shape, NEG_INF, F32)

        def head_group(hg, carry):
            for u in range(HEAD_UNROLL):
                head(hg * HEAD_UNROLL + u)
            return carry

        def head(h):
            raw, top = [], []
            for p in range(2):
                col = pl.multiple_of((2 * h + p) * PEER_NK, PEER_NK)
                s = _dot_nt(keys_ref[p].astype(BF16), q_ref[:, pl.ds(col, PEER_NK)])
                raw.append(s)
                top.append(_top_values_sorted_lists(s, N_RANK))
            for idx, (a, b) in enumerate(pairs):
                cand_scr[idx:idx + 1, :] = top[0][a] + top[1][b]
            best = _top_values(cand_scr[...], N_RANK)
            z = jnp.ones_like(best[0])
            for i in range(1, PEER_TOPK):
                z = z + jnp.exp(best[i] - best[0])
            log_z = jnp.log(z)
            mid = 0.5 * (best[PEER_TOPK - 1] + best[PEER_TOPK])
            ab_ref[2 * h] = (raw[0] - top[0][0]) * LOG2E
            ab_ref[2 * h + 1] = (raw[1] - top[1][0] - log_z) * LOG2E + GATE_SHIFT
            thr_ref[pl.ds(h, 1), :] = (mid - best[0] - log_z) * LOG2E + GATE_SHIFT

        lax.fori_loop(0, PEER_HEADS // HEAD_UNROLL, head_group, 0)

    return pl.pallas_call(
        body,
        grid=(n_tok // tt,),
        in_specs=[
            pl.BlockSpec((tt, q.shape[1]), lambda i: (i, 0)),
            pl.BlockSpec(keys.shape, lambda i: (0, 0, 0)),
        ],
        out_specs=[
            pl.BlockSpec((2 * PEER_HEADS, PEER_NK, tt), lambda i: (0, 0, i)),
            pl.BlockSpec((PEER_HEADS, tt), lambda i: (0, i)),
        ],
        out_shape=[
            jax.ShapeDtypeStruct((2 * PEER_HEADS, PEER_NK, n_tok), F32),
            jax.ShapeDtypeStruct((PEER_HEADS, n_tok), F32),
        ],
        scratch_shapes=[pltpu.VMEM((n_cand, tt), F32)],
        compiler_params=_params("parallel"),
        name="peer_topk",
    )(q, keys)


def _peer_dense(xt, u_tab, v_tab, layer, ab, thr, res, gate, *, mod_row, tt, et, out_norm_g=None, lane_chunk=128):
    d, n_tok = xt.shape
    n_exp = u_tab.shape[1]
    rows_per_step = et // PEER_NK
    n_blk = n_exp // et
    n_lane = tt // lane_chunk
    assert rows_per_step == 8

    def body(xt_ref, u_ref, v_ref, ab_ref, thr_ref, res_ref, gate_ref, *rest):
        ng_ref = rest[0] if out_norm_g is not None else None
        o_ref, acc_scr, act_scr, w_scr = rest[-4:]
        s = pl.program_id(1)

        @pl.when(s == 0)
        def _():
            acc_scr[...] = jnp.zeros_like(acc_scr)

        grp = pl.multiple_of(s * rows_per_step, rows_per_step)
        act_scr[...] = _dot(u_ref[...], xt_ref[...])
        for c in range(n_lane):
            cs = slice(c * lane_chunk, (c + 1) * lane_chunk)
            for r in range(rows_per_step):
                rs = slice(r * PEER_NK, (r + 1) * PEER_NK)
                g = jnp.zeros((PEER_NK, lane_chunk), F32)
                for h in range(PEER_HEADS):
                    a_row = ab_ref[2 * h, pl.ds(grp, rows_per_step), cs][r:r + 1]
                    sc = a_row + ab_ref[2 * h + 1, :, cs]
                    g = g + jnp.where(sc >= thr_ref[h:h + 1, cs], jnp.exp2(sc), 0.0)
                a = act_scr[rs, cs]
                w_scr[rs, cs] = (g * (a * (1.0 + lax.erf(a * SQRT_HALF)))).astype(BF16)
        acc_scr[...] += lax.dot_general(w_scr[...], v_ref[...], (((0,), (0,)), ((), ())),
                                        preferred_element_type=F32)

        @pl.when(s == n_blk - 1)
        def _():
            y = res_ref[...] + gate_ref[...] * acc_scr[...]
            if ng_ref is not None:
                y = y * lax.rsqrt(jnp.mean(y * y, axis=-1, keepdims=True) + NORM_EPS) * ng_ref[...]
            o_ref[...] = y

    extra_args = [] if out_norm_g is None else [out_norm_g.reshape(1, d)]
    extra_specs = [] if out_norm_g is None else [pl.BlockSpec((1, d), lambda i, s: (0, 0))]
    return pl.pallas_call(
        body,
        grid=(n_tok // tt, n_blk),
        in_specs=[
            pl.BlockSpec((d, tt), lambda i, s: (0, i)),
            pl.BlockSpec((None, et, d), lambda i, s: (layer, s, 0)),
            pl.BlockSpec((None, et, d), lambda i, s: (layer, s, 0)),
            pl.BlockSpec((2 * PEER_HEADS, PEER_NK, tt), lambda i, s: (0, 0, i)),
            pl.BlockSpec((PEER_HEADS, tt), lambda i, s: (0, i)),
            pl.BlockSpec((tt, d), lambda i, s: (i, 0)),
            pl.BlockSpec((None, 1, d), lambda i, s: (mod_row(i), 0, 0)),
        ] + extra_specs,
        out_specs=pl.BlockSpec((tt, d), lambda i, s: (i, 0)),
        out_shape=jax.ShapeDtypeStruct((n_tok, d), F32),
        scratch_shapes=[pltpu.VMEM((tt, d), F32), pltpu.VMEM((et, tt), F32), pltpu.VMEM((et, tt), BF16)],
        compiler_params=_params("parallel", "arbitrary"),
        name="peer_dense",
    )(xt, u_tab, v_tab, ab, thr, res, gate, *extra_args)


def _cast_tables(tabs, *, rows=1024):
    n_l, n_e, d = tabs[0].shape

    def body(*refs):
        for src, dst in zip(refs[:len(tabs)], refs[len(tabs):]):
            dst[...] = src[...].astype(BF16)

    spec = pl.BlockSpec((None, rows, d), lambda l, i: (l, i, 0))
    return pl.pallas_call(
        body,
        grid=(n_l, n_e // rows),
        in_specs=[spec] * len(tabs),
        out_specs=[spec] * len(tabs),
        out_shape=[jax.ShapeDtypeStruct((n_l, n_e, d), BF16)] * len(tabs),
        compiler_params=_params("parallel", "parallel"),
        name="cast_tables",
    )(*tabs)


TM = 1024
PEER_TT = 512
PEER_ET = 1024
TOPK_TT = 256


def kernel(x_prompt, x_sample, cache_k, cache_v, c, c_ctx, norm1_g, norm2_g, final_g,
           ada_w, ada_b, w_in_e, w_out_e, lam_q1, lam_k1, lam_q2, lam_k2, subln_g,
           pool_w, pool_scale, pw1_w, pw1_b, dw_w, dw_b, cln_g, cln_b, pw2_w, pw2_b,
           peer_wq, peer_keys, peer_u, peer_v):
    d = D_MODEL
    bp, tp, _ = x_prompt.shape
    bs, ts, _ = x_sample.shape

    cvec = jnp.zeros((MOD_ROWS, d), F32).at[0].set(c_ctx).at[1:1 + bs].set(c)
    mods = _ada(cvec, ada_w, ada_b)
    mods = mods.reshape(DEPTH, MOD_ROWS, 6, 1, d).transpose(0, 2, 1, 3, 4)

    def sample_row(tile):
        return lambda i: 1 + (i * tile) // ts

    streams = [
        dict(x=x_prompt.reshape(bp * tp, d), b=bp, t=tp, mod_row=lambda tile: (lambda i: 0)),
        dict(x=x_sample.reshape(bs * ts, d), b=bs, t=ts, mod_row=sample_row),
    ]
    rope = _rope_tables(ts)
    new_k, new_v = [], []
    u_all, v_all = _cast_tables([peer_u, peer_v])

    for i in range(DEPTH):
        j = i // 2
        sh1, sc1, g1, sh2, sc2, g2 = (mods[i, m] for m in range(6))
        wq = peer_wq[i].astype(BF16)
        if i % 2 == 0:
            lam_init = 0.8 - 0.6 * math.exp(-0.3 * i)
            w_in = w_in_e[j].astype(BF16)
            w_out = w_out_e[j].astype(BF16)
            pw = pool_w[j].astype(BF16)
            lamv = jnp.zeros((8, LANES), F32)
            lamv = lamv.at[0, :A_HD].set(lam_q1[j]).at[1, :A_HD].set(lam_k1[j])
            lamv = lamv.at[2, :A_HD].set(lam_q2[j]).at[3, :A_HD].set(lam_k2[j])
        else:
            w1 = pw1_w[j].astype(BF16)
            w2 = pw2_w[j].astype(BF16)

        for si, st in enumerate(streams):
            x, mod_row = st["x"], st["mod_row"](TM)
            if i % 2 == 0:
                proj = _linear([x], w_in, mod_row=mod_row, tm=TM, tn=1024, prologue="rms_mod",
                               norm_g=norm1_g[i], shift=sh1, scale=sc1, name="w_in")
                proj3 = proj.reshape(st["b"], st["t"], IN_EVEN)
                if si == 0:
                    new_k.append(proj3[:, :, A_QK:2 * A_QK].reshape(bp, tp, A_HEADS, 2, A_HD))
                    new_v.append(proj3[:, :, 2 * A_QK:2 * A_QK + A_V].reshape(bp, tp, A_HEADS, 2 * A_HD))
                    att = _attention(proj3, lamv, subln_g[j], lam_init, tq=tp, heads_per_step=4)
                else:
                    ctx = (cache_k[:, j].reshape(bs, -1, A_QK), cache_v[:, j].reshape(bs, -1, A_V))
                    att = _attention(proj3, lamv, subln_g[j], lam_init, tq=1024, ctx=ctx, rope=rope)
                pooled = _pool(proj3, pw, pool_scale[j])
                x = _linear([att.reshape(-1, A_V), pooled.reshape(-1, POOL_WIDTH)], w_out,
                            mod_row=mod_row, tm=TM, tn=1024, epilogue="resid", res=x, gate=g1,
                            name="w_out")
            else:
                u = _linear([x], w1, mod_row=mod_row, tm=TM, tn=1024, prologue="rms_mod",
                            epilogue="glu", bias=pw1_b[j], norm_g=norm1_g[i], shift=sh1, scale=sc1,
                            out_dtype=BF16, name="pw1_glu")
                u = _dwconv(u.reshape(st["b"], st["t"], d), dw_w[j], dw_b[j]).reshape(-1, d)
                x = _linear([u], w2, mod_row=mod_row, tm=TM, tn=1024, prologue="ln_silu",
                            epilogue="resid", bias=pw2_b[j], norm_g=cln_g[j], norm_b=cln_b[j],
                            res=x, gate=g1, name="pw2")

            q, xt = _linear([x], wq, mod_row=mod_row, tm=TM, tn=1024, prologue="rms_mod",
                            norm_g=norm2_g[i], shift=sh2, scale=sc2, emit_xt=True, out_dtype=BF16,
                            name="peer_q")
            ab, thr = _peer_topk(q, peer_keys[i], tt=TOPK_TT)
            x = _peer_dense(xt, u_all, v_all, i, ab, thr, x, g2, mod_row=st["mod_row"](PEER_TT),
                            tt=PEER_TT, et=PEER_ET, out_norm_g=final_g if i == DEPTH - 1 else None)
            st["x"] = x

    y_prompt = streams[0]["x"].reshape(bp, tp, d)
    y_sample = streams[1]["x"].reshape(bs, ts, d)
    return (y_prompt, y_sample, jnp.stack(new_k, axis=1), jnp.stack(new_v, axis=1))
```

```python
import math

import jax
import jax.numpy as jnp
from jax import lax
from jax.experimental import pallas as pl
from jax.experimental.pallas import tpu as pltpu

F32 = jnp.float32
BF16 = jnp.bfloat16

D_MODEL = 2048
DEPTH = 2
GRID_W = 64
A_HD = 64
A_HEADS = D_MODEL // (4 * A_HD)
A_QK = A_HEADS * 2 * A_HD
A_V = A_HEADS * 2 * A_HD
POOL_WIDTH = D_MODEL // 2
POOL_WINDOWS = (2, 4, 8, 16)
POOL_C = POOL_WIDTH // len(POOL_WINDOWS)
IN_EVEN = 2 * A_QK + A_V + POOL_WIDTH
CONV_K = 31
PEER_HEADS = 8
PEER_NK = 128
PEER_N = PEER_NK * PEER_NK
PEER_TOPK = 16
ROPE_THETA = 10000.0
ROPE_AXIS_F = A_HD // 4
NORM_EPS = 1e-6
LN_EPS = 1e-5

LANES = 128
MOD_ROWS = 16
VMEM_LIMIT = 56 * 1024 * 1024
HALO = 16
LOG2E = 1.4426950408889634
SQRT_HALF = 0.7071067811865476
NEG_INF = float("-inf")


def _params(*sem):
    return pltpu.CompilerParams(dimension_semantics=sem, vmem_limit_bytes=VMEM_LIMIT)


def _dot(a, b):
    return jnp.dot(a, b, preferred_element_type=F32)


def _dot_nt(a, b):
    return lax.dot_general(a, b, (((1,), (1,)), ((), ())), preferred_element_type=F32)


def _ada_body(c_ref, w_ref, b_ref, o_ref):
    c = c_ref[...]
    s = (c * jax.nn.sigmoid(c)).astype(BF16)
    o_ref[...] = _dot(s, w_ref[...].astype(BF16)) + b_ref[...]


def _ada(cvec, ada_w, ada_b):
    depth, d, n = ada_w.shape
    tn = 1024
    return pl.pallas_call(
        _ada_body,
        grid=(depth, n // tn),
        in_specs=[
            pl.BlockSpec((MOD_ROWS, d), lambda l, j: (0, 0)),
            pl.BlockSpec((None, d, tn), lambda l, j: (l, 0, j)),
            pl.BlockSpec((None, 1, tn), lambda l, j: (l, 0, j)),
        ],
        out_specs=pl.BlockSpec((None, MOD_ROWS, tn), lambda l, j: (l, 0, j)),
        out_shape=jax.ShapeDtypeStruct((depth, MOD_ROWS, n), F32),
        compiler_params=_params("parallel", "parallel"),
        name="ada",
    )(cvec, ada_w, ada_b.reshape(depth, 1, n))


def _linear(xs, w, *, mod_row, tm, tn, prologue="none", epilogue="none", bias=None,
            norm_g=None, norm_b=None, shift=None, scale=None, res=None, gate=None,
            emit_xt=False, out_dtype=F32, name="linear"):
    n_tok = xs[0].shape[0]
    k_tot = sum(x.shape[1] for x in xs)
    n_w = w.shape[1]
    n_out = n_w // 2 if epilogue == "glu" else n_w
    nj = n_out // tn
    row_map = lambda i, j: (i, 0)
    vec_map = lambda i, j: (0, 0)
    mod_map = lambda i, j: (mod_row(i), 0, 0)

    args, specs = [], []
    for x in xs:
        args.append(x)
        specs.append(pl.BlockSpec((tm, x.shape[1]), row_map))
    if prologue == "rms_mod":
        args += [norm_g.reshape(1, k_tot), shift, scale]
        specs += [pl.BlockSpec((1, k_tot), vec_map),
                  pl.BlockSpec((None, 1, k_tot), mod_map),
                  pl.BlockSpec((None, 1, k_tot), mod_map)]
    elif prologue == "ln_silu":
        args += [norm_g.reshape(1, k_tot), norm_b.reshape(1, k_tot)]
        specs += [pl.BlockSpec((1, k_tot), vec_map)] * 2
    args.append(w)
    specs.append(pl.BlockSpec((k_tot, tn), lambda i, j: (0, j)))
    if epilogue == "glu":
        args.append(w)
        specs.append(pl.BlockSpec((k_tot, tn), lambda i, j: (0, j + nj)))
    if bias is not None:
        b2 = bias.reshape(1, n_w)
        args.append(b2)
        specs.append(pl.BlockSpec((1, tn), lambda i, j: (0, j)))
        if epilogue == "glu":
            args.append(b2)
            specs.append(pl.BlockSpec((1, tn), lambda i, j: (0, j + nj)))
    if epilogue == "resid":
        args += [res, gate]
        specs += [pl.BlockSpec((tm, tn), lambda i, j: (i, j)),
                  pl.BlockSpec((None, 1, tn), lambda i, j: (mod_row(i), 0, j))]

    out_shape = [jax.ShapeDtypeStruct((n_tok, n_out), out_dtype)]
    out_specs = [pl.BlockSpec((tm, tn), lambda i, j: (i, j))]
    if emit_xt:
        out_shape.append(jax.ShapeDtypeStruct((k_tot, n_tok), BF16))
        out_specs.append(pl.BlockSpec((k_tot, tm), lambda i, j: (0, i)))
    n_x = len(xs)
    has_bias = bias is not None

    def body(*refs):
        it = iter(refs)
        x_refs = [next(it) for _ in range(n_x)]
        if prologue == "rms_mod":
            g_ref, sh_ref, sc_ref = next(it), next(it), next(it)
        elif prologue == "ln_silu":
            g_ref, b_ref = next(it), next(it)
        w_ref = next(it)
        w2_ref = next(it) if epilogue == "glu" else None
        bias_ref = next(it) if has_bias else None
        bias2_ref = next(it) if (has_bias and epilogue == "glu") else None
        if epilogue == "resid":
            res_ref, gate_ref = next(it), next(it)
        o_ref = next(it)
        xt_ref = next(it) if emit_xt else None
        xm_scr = next(it)

        @pl.when(pl.program_id(1) == 0)
        def _():
            if prologue == "none":
                off = 0
                for xr in x_refs:
                    kk = xr.shape[1]
                    xm_scr[:, off:off + kk] = xr[...].astype(BF16)
                    off += kk
                return
            x = x_refs[0][...].astype(F32)
            if prologue == "rms_mod":
                y = x * lax.rsqrt(jnp.mean(x * x, axis=-1, keepdims=True) + NORM_EPS)
                y = y * g_ref[...]
                y = y * (1.0 + sc_ref[...]) + sh_ref[...]
            else:
                mu = jnp.mean(x, axis=-1, keepdims=True)
                xc = x - mu
                y = xc * lax.rsqrt(jnp.mean(xc * xc, axis=-1, keepdims=True) + LN_EPS)
                y = y * g_ref[...] + b_ref[...]
                y = y * jax.nn.sigmoid(y)
            xm_scr[...] = y.astype(BF16)
            if emit_xt:
                xt_ref[...] = y.T.astype(BF16)

        xm = xm_scr[...]
        y = _dot(xm, w_ref[...])
        if has_bias:
            y = y + bias_ref[...]
        if epilogue == "glu":
            gte = _dot(xm, w2_ref[...])
            if has_bias:
                gte = gte + bias2_ref[...]
            y = y * jax.nn.sigmoid(gte)
        elif epilogue == "resid":
            y = res_ref[...] + gate_ref[...] * y
        o_ref[...] = y.astype(o_ref.dtype)

    outs = pl.pallas_call(
        body,
        grid=(n_tok // tm, nj),
        in_specs=specs,
        out_specs=out_specs,
        out_shape=out_shape,
        scratch_shapes=[pltpu.VMEM((tm, k_tot), BF16)],
        compiler_params=_params("parallel", "arbitrary"),
        name=name,
    )(*args)
    return outs if emit_xt else outs[0]


def _rope_tables(n_tokens):
    t = jnp.arange(n_tokens)
    pos = jnp.stack([(t // GRID_W).astype(F32), (t % GRID_W).astype(F32)], axis=1)
    freq = ROPE_THETA ** (-jnp.arange(ROPE_AXIS_F, dtype=F32) / ROPE_AXIS_F)
    lane = jnp.arange(LANES)
    axis = (lane % A_HD) // (2 * ROPE_AXIS_F)
    half = (lane // ROPE_AXIS_F) % 2
    ang = pos[:, axis] * freq[lane % ROPE_AXIS_F][None, :]
    cos, sin = jnp.cos(ang), jnp.sin(ang)
    sin_lo = jnp.where(half[None, :] == 1, sin, 0.0)
    sin_hi = jnp.where(half[None, :] == 0, -sin, 0.0)
    return cos, sin_lo, sin_hi


def _rope(x, cos, sin_lo, sin_hi):
    return (x * cos + pltpu.roll(x, ROPE_AXIS_F, 1) * sin_lo
            + pltpu.roll(x, LANES - ROPE_AXIS_F, 1) * sin_hi)


def _attention(proj, lamv, subln_g, lam_init, *, tq, ctx=None, rope=None, heads_per_step=1):
    bsz, t_len, _ = proj.shape
    nq = t_len // tq
    has_ctx = ctx is not None
    out_scale = 1.0 - lam_init
    hb = heads_per_step
    assert A_HEADS % hb == 0 and (hb == 1 or not has_ctx)

    def body(*refs):
        it = iter(refs)
        lam_ref, g_ref, q_ref, k_ref, v_ref = (next(it) for _ in range(5))
        if has_ctx:
            ck_ref, cv_ref = next(it), next(it)
            cq_ref, slq_ref, shq_ref, ck_tab, slk_tab, shk_tab = (next(it) for _ in range(6))
        o_ref = next(it)
        if has_ctx:
            kr_scr = next(it)

            @pl.when(pl.program_id(2) == 0)
            def _():
                kr_scr[...] = _rope(k_ref[...], ck_tab[...], slk_tab[...], shk_tab[...]).astype(BF16)

        lv = lam_ref[...]
        lam = (jnp.exp(jnp.sum(lv[0:1] * lv[1:2], axis=-1, keepdims=True))
               - jnp.exp(jnp.sum(lv[2:3] * lv[3:4], axis=-1, keepdims=True)) + lam_init)
        for hd in range(hb):
            cols = slice(hd * LANES, (hd + 1) * LANES)
            q = q_ref[:, cols]
            if has_ctx:
                q = _rope(q, cq_ref[...], slq_ref[...], shq_ref[...])
                k_own = kr_scr[...]
            else:
                k_own = k_ref[:, cols].astype(BF16)
            q = q * (A_HD ** -0.5 * LOG2E)
            lane = lax.broadcasted_iota(jnp.int32, q.shape, 1)
            v_own = v_ref[:, cols].astype(BF16)
            if has_ctx:
                k_ctx = ck_ref[...].astype(BF16)
                v_ctx = cv_ref[...].astype(BF16)

            parts = []
            for comp in range(2):
                in_comp = (lane < A_HD) if comp == 0 else (lane >= A_HD)
                qc = jnp.where(in_comp, q, 0.0).astype(BF16)
                s_own = _dot_nt(qc, k_own)
                m = jnp.max(s_own, axis=-1, keepdims=True)
                if has_ctx:
                    s_ctx = _dot_nt(qc, k_ctx)
                    m = jnp.maximum(m, jnp.max(s_ctx, axis=-1, keepdims=True))
                e_own = jnp.exp2(s_own - m)
                l = jnp.sum(e_own, axis=-1, keepdims=True)
                pv = _dot(e_own.astype(BF16), v_own)
                if has_ctx:
                    e_ctx = jnp.exp2(s_ctx - m)
                    l = l + jnp.sum(e_ctx, axis=-1, keepdims=True)
                    pv = pv + _dot(e_ctx.astype(BF16), v_ctx)
                parts.append(pv * (1.0 / l))
            out = parts[0] - lam * parts[1]
            out = out * lax.rsqrt(jnp.mean(out * out, axis=-1, keepdims=True) + NORM_EPS)
            o_ref[:, cols] = (out * g_ref[...] * out_scale).astype(o_ref.dtype)

    nh = A_HEADS // hb
    wid = hb * LANES
    args = [lamv, subln_g.reshape(1, 2 * A_HD), proj, proj, proj]
    specs = [
        pl.BlockSpec((8, LANES), lambda b, h, i: (0, 0)),
        pl.BlockSpec((1, 2 * A_HD), lambda b, h, i: (0, 0)),
        pl.BlockSpec((None, tq, wid), lambda b, h, i: (b, i, h)),
        pl.BlockSpec((None, t_len, wid), lambda b, h, i: (b, 0, nh + h)),
        pl.BlockSpec((None, t_len, wid), lambda b, h, i: (b, 0, 2 * nh + h)),
    ]
    scratch = []
    if has_ctx:
        p_len = ctx[0].shape[1]
        args += [ctx[0], ctx[1]]
        specs += [pl.BlockSpec((None, p_len, LANES), lambda b, h, i: (b, 0, h))] * 2
        args += list(rope) + list(rope)
        specs += [pl.BlockSpec((tq, LANES), lambda b, h, i: (i, 0))] * 3
        specs += [pl.BlockSpec((t_len, LANES), lambda b, h, i: (0, 0))] * 3
        scratch = [pltpu.VMEM((t_len, LANES), BF16)]
    return pl.pallas_call(
        body,
        grid=(bsz, nh, nq),
        in_specs=specs,
        out_specs=pl.BlockSpec((None, tq, wid), lambda b, h, i: (b, i, h)),
        out_shape=jax.ShapeDtypeStruct((bsz, t_len, A_V), BF16),
        scratch_shapes=scratch,
        compiler_params=_params("parallel", "parallel", "arbitrary"),
        name="attention_ctx" if has_ctx else "attention",
    )(*args)


def _pool(proj, pool_w, pool_scale):
    bsz, t_len, n_in = proj.shape
    col_blk = (n_in - POOL_WIDTH) // POOL_WIDTH

    def body(p_ref, w_ref, sc_ref, o_ref, pad_scr):
        zeros = jnp.zeros((HALO, POOL_C), F32)
        pad_scr[0:HALO, :] = zeros
        pad_scr[HALO + t_len:2 * HALO + t_len, :] = zeros
        tok = lax.broadcasted_iota(jnp.int32, (t_len, POOL_C), 0)
        for g, win in enumerate(POOL_WINDOWS):
            cols = slice(g * POOL_C, (g + 1) * POOL_C)
            x = p_ref[:, cols]
            pad_scr[HALO:HALO + t_len, :] = x
            back, fwd = win // 2, win - win // 2 - 1
            total = pad_scr[pl.ds(HALO - back, t_len), :]
            for o in range(-back + 1, fwd + 1):
                total = total + pad_scr[pl.ds(HALO + o, t_len), :]
            lo = jnp.maximum(tok - back, 0)
            hi = jnp.minimum(tok + fwd, t_len - 1)
            cnt = (hi - lo + 1).astype(F32)
            diff = (total / cnt - x).astype(BF16)
            o_ref[:, cols] = (_dot(diff, w_ref[g]) * sc_ref[:, cols]).astype(o_ref.dtype)

    return pl.pallas_call(
        body,
        grid=(bsz,),
        in_specs=[
            pl.BlockSpec((None, t_len, POOL_WIDTH), lambda b: (b, 0, col_blk)),
            pl.BlockSpec(pool_w.shape, lambda b: (0, 0, 0)),
            pl.BlockSpec((1, POOL_WIDTH), lambda b: (0, 0)),
        ],
        out_specs=pl.BlockSpec((None, t_len, POOL_WIDTH), lambda b: (b, 0, 0)),
        out_shape=jax.ShapeDtypeStruct((bsz, t_len, POOL_WIDTH), BF16),
        scratch_shapes=[pltpu.VMEM((t_len + 2 * HALO, POOL_C), F32)],
        compiler_params=_params("parallel"),
        name="pool",
    )(proj, pool_w, pool_scale.reshape(1, POOL_WIDTH))


def _dwconv(u, dw_w, dw_b, *, cb=256, rows=128):
    bsz, t_len, ch = u.shape
    taps = dw_w.shape[0]
    reach = taps // 2
    assert reach <= HALO and taps <= 32
    ext = t_len + 2 * HALO - 8

    def body(u_ref, w_ref, b_ref, o_ref, pad_scr, sh_scr):
        zeros = jnp.zeros((HALO, cb), F32)
        pad_scr[0:HALO, :] = zeros
        pad_scr[HALO + t_len:2 * HALO + t_len, :] = zeros
        pad_scr[HALO:HALO + t_len, :] = u_ref[...].astype(F32)
        for b in range(8):
            off = b + HALO - reach
            sh_scr[b] = pad_scr[off:off + ext, :]

        def chunk(r, carry):
            base = pl.multiple_of(r * rows, rows)
            acc = jnp.zeros((rows, cb), F32)
            for k in range(taps):
                a, b = divmod(k, 8)
                acc = acc + sh_scr[b, pl.ds(base + 8 * a, rows), :] * w_ref[k:k + 1, :]
            o_ref[pl.ds(base, rows), :] = (acc + b_ref[...]).astype(o_ref.dtype)
            return carry

        lax.fori_loop(0, t_len // rows, chunk, 0)

    w_pad = jnp.zeros((32, ch), F32).at[:taps].set(dw_w)
    return pl.pallas_call(
        body,
        grid=(bsz, ch // cb),
        in_specs=[
            pl.BlockSpec((None, t_len, cb), lambda b, c: (b, 0, c)),
            pl.BlockSpec((32, cb), lambda b, c: (0, c)),
            pl.BlockSpec((1, cb), lambda b, c: (0, c)),
        ],
        out_specs=pl.BlockSpec((None, t_len, cb), lambda b, c: (b, 0, c)),
        out_shape=jax.ShapeDtypeStruct((bsz, t_len, ch), u.dtype),
        scratch_shapes=[pltpu.VMEM((t_len + 2 * HALO, cb), F32), pltpu.VMEM((8, ext, cb), F32)],
        compiler_params=_params("parallel", "parallel"),
        name="dwconv",
    )(u, w_pad, dw_b.reshape(1, ch))


N_RANK = PEER_TOPK + 1
HEAD_UNROLL = 4
GATE_SHIFT = -1.0


def _top_values(s, count):
    vals = []
    for _ in range(count):
        m = jnp.max(s, axis=0, keepdims=True)
        vals.append(m)
        s = jnp.where(s == m, NEG_INF, s)
    return vals


def _sorting_network(n):
    pairs = []
    p = 1
    while p < n:
        k = p
        while k >= 1:
            for j in range(k % p, n - k, 2 * k):
                for i in range(min(k, n - j - k)):
                    if (i + j) // (2 * p) == (i + j + k) // (2 * p):
                        pairs.append((i + j, i + j + k))
            k //= 2
        p *= 2
    return pairs


def _top_values_sorted_lists(s, count):
    rows, cols = s.shape
    n_slab = rows // 8
    s3 = s.reshape(n_slab, 8, cols)
    v = [s3[j] for j in range(n_slab)]
    for i, j in _sorting_network(n_slab):
        hi, lo = jnp.maximum(v[i], v[j]), jnp.minimum(v[i], v[j])
        v[i], v[j] = hi, lo
    vals = []
    for a in range(count):
        m = jnp.max(v[0], axis=0, keepdims=True)
        vals.append(m)
        adv = v[0] == m
        for j in range(min(count - 1 - a, n_slab)):
            nxt = v[j + 1] if j + 1 < n_slab else NEG_INF
            v[j] = jnp.where(adv, nxt, v[j])
    return vals


def _peer_topk(q, keys, *, tt):
    n_tok = q.shape[0]
    pairs = [(a, b) for a in range(N_RANK) for b in range(N_RANK) if (a + 1) * (b + 1) <= N_RANK]

    n_cand = -(-len(pairs) // 8) * 8

    def body(q_ref, keys_ref, ab_ref, thr_ref, cand_scr):
        cand_scr[...] = jnp.full(cand_scr.shape, NEG_INF, F32)

        def head_group(hg, carry):
            for u in range(HEAD_UNROLL):
                head(hg * HEAD_UNROLL + u)
            return carry

        def head(h):
            raw, top = [], []
            for p in range(2):
                col = pl.multiple_of((2 * h + p) * PEER_NK, PEER_NK)
                s = _dot_nt(keys_ref[p].astype(BF16), q_ref[:, pl.ds(col, PEER_NK)])
                raw.append(s)
                top.append(_top_values_sorted_lists(s, N_RANK))
            for idx, (a, b) in enumerate(pairs):
                cand_scr[idx:idx + 1, :] = top[0][a] + top[1][b]
            best = _top_values(cand_scr[...], N_RANK)
            z = jnp.ones_like(best[0])
            for i in range(1, PEER_TOPK):
                z = z + jnp.exp(best[i] - best[0])
            log_z = jnp.log(z)
            mid = 0.5 * (best[PEER_TOPK - 1] + best[PEER_TOPK])
            ab_ref[2 * h] = (raw[0] - top[0][0]) * LOG2E
            ab_ref[2 * h + 1] = (raw[1] - top[1][0] - log_z) * LOG2E + GATE_SHIFT
            thr_ref[pl.ds(h, 1), :] = (mid - best[0] - log_z) * LOG2E + GATE_SHIFT

        lax.fori_loop(0, PEER_HEADS // HEAD_UNROLL, head_group, 0)

    return pl.pallas_call(
        body,
        grid=(n_tok // tt,),
        in_specs=[
            pl.BlockSpec((tt, q.shape[1]), lambda i: (i, 0)),
            pl.BlockSpec(keys.shape, lambda i: (0, 0, 0)),
        ],
        out_specs=[
            pl.BlockSpec((2 * PEER_HEADS, PEER_NK, tt), lambda i: (0, 0, i)),
            pl.BlockSpec((PEER_HEADS, tt), lambda i: (0, i)),
        ],
        out_shape=[
            jax.ShapeDtypeStruct((2 * PEER_HEADS, PEER_NK, n_tok), F32),
            jax.ShapeDtypeStruct((PEER_HEADS, n_tok), F32),
        ],
        scratch_shapes=[pltpu.VMEM((n_cand, tt), F32)],
        compiler_params=_params("parallel"),
        name="peer_topk",
    )(q, keys)


def _peer_dense(xt, u_tab, v_tab, layer, ab, thr, res, gate, *, mod_row, tt, et, out_norm_g=None, lane_chunk=128):
    d, n_tok = xt.shape
    n_exp = u_tab.shape[1]
    rows_per_step = et // PEER_NK
    n_blk = n_exp // et
    n_lane = tt // lane_chunk
    assert rows_per_step == 8

    def body(xt_ref, u_ref, v_ref, ab_ref, thr_ref, res_ref, gate_ref, *rest):
        ng_ref = rest[0] if out_norm_g is not None else None
        o_ref, acc_scr, act_scr, w_scr = rest[-4:]
        s = pl.program_id(1)

        @pl.when(s == 0)
        def _():
            acc_scr[...] = jnp.zeros_like(acc_scr)

        grp = pl.multiple_of(s * rows_per_step, rows_per_step)
        for r in range(rows_per_step):
            rs = slice(r * PEER_NK, (r + 1) * PEER_NK)
            act = _dot(u_ref[rs, :], xt_ref[...])
            for c in range(n_lane):
                cs = slice(c * lane_chunk, (c + 1) * lane_chunk)
                g = jnp.zeros((PEER_NK, lane_chunk), F32)
                for h in range(PEER_HEADS):
                    a_row = ab_ref[2 * h, pl.ds(grp, rows_per_step), cs][r:r + 1]
                    sc = a_row + ab_ref[2 * h + 1, :, cs]
                    g = g + jnp.where(sc >= thr_ref[h:h + 1, cs], jnp.exp2(sc), 0.0)
                a = act[:, cs]
                w_scr[rs, cs] = (g * (a * (1.0 + lax.erf(a * SQRT_HALF)))).astype(BF16)
        acc_scr[...] += lax.dot_general(w_scr[...], v_ref[...], (((0,), (0,)), ((), ())),
                                        preferred_element_type=F32)

        @pl.when(s == n_blk - 1)
        def _():
            y = res_ref[...] + gate_ref[...] * acc_scr[...]
            if ng_ref is not None:
                y = y * lax.rsqrt(jnp.mean(y * y, axis=-1, keepdims=True) + NORM_EPS) * ng_ref[...]
            o_ref[...] = y

    extra_args = [] if out_norm_g is None else [out_norm_g.reshape(1, d)]
    extra_specs = [] if out_norm_g is None else [pl.BlockSpec((1, d), lambda i, s: (0, 0))]
    return pl.pallas_call(
        body,
        grid=(n_tok // tt, n_blk),
        in_specs=[
            pl.BlockSpec((d, tt), lambda i, s: (0, i)),
            pl.BlockSpec((None, et, d), lambda i, s: (layer, s, 0)),
            pl.BlockSpec((None, et, d), lambda i, s: (layer, s, 0)),
            pl.BlockSpec((2 * PEER_HEADS, PEER_NK, tt), lambda i, s: (0, 0, i)),
            pl.BlockSpec((PEER_HEADS, tt), lambda i, s: (0, i)),
            pl.BlockSpec((tt, d), lambda i, s: (i, 0)),
            pl.BlockSpec((None, 1, d), lambda i, s: (mod_row(i), 0, 0)),
        ] + extra_specs,
        out_specs=pl.BlockSpec((tt, d), lambda i, s: (i, 0)),
        out_shape=jax.ShapeDtypeStruct((n_tok, d), F32),
        scratch_shapes=[pltpu.VMEM((tt, d), F32), pltpu.VMEM((et, tt), F32), pltpu.VMEM((et, tt), BF16)],
        compiler_params=_params("parallel", "arbitrary"),
        name="peer_dense",
    )(xt, u_tab, v_tab, ab, thr, res, gate, *extra_args)


def _cast_tables(tabs, *, rows=1024):
    n_l, n_e, d = tabs[0].shape

    def body(*refs):
        for src, dst in zip(refs[:len(tabs)], refs[len(tabs):]):
            dst[...] = src[...].astype(BF16)

    spec = pl.BlockSpec((None, rows, d), lambda l, i: (l, i, 0))
    return pl.pallas_call(
        body,
        grid=(n_l, n_e // rows),
        in_specs=[spec] * len(tabs),
        out_specs=[spec] * len(tabs),
        out_shape=[jax.ShapeDtypeStruct((n_l, n_e, d), BF16)] * len(tabs),
        compiler_params=_params("parallel", "parallel"),
        name="cast_tables",
    )(*tabs)


TM = 1024
PEER_TT = 512
PEER_ET = 1024
TOPK_TT = 256


def kernel(x_prompt, x_sample, cache_k, cache_v, c, c_ctx, norm1_g, norm2_g, final_g,
           ada_w, ada_b, w_in_e, w_out_e, lam_q1, lam_k1, lam_q2, lam_k2, subln_g,
           pool_w, pool_scale, pw1_w, pw1_b, dw_w, dw_b, cln_g, cln_b, pw2_w, pw2_b,
           peer_wq, peer_keys, peer_u, peer_v):
    d = D_MODEL
    bp, tp, _ = x_prompt.shape
    bs, ts, _ = x_sample.shape

    cvec = jnp.zeros((MOD_ROWS, d), F32).at[0].set(c_ctx).at[1:1 + bs].set(c)
    mods = _ada(cvec, ada_w, ada_b)
    mods = mods.reshape(DEPTH, MOD_ROWS, 6, 1, d).transpose(0, 2, 1, 3, 4)

    def sample_row(tile):
        return lambda i: 1 + (i * tile) // ts

    streams = [
        dict(x=x_prompt.reshape(bp * tp, d), b=bp, t=tp, mod_row=lambda tile: (lambda i: 0)),
        dict(x=x_sample.reshape(bs * ts, d), b=bs, t=ts, mod_row=sample_row),
    ]
    rope = _rope_tables(ts)
    new_k, new_v = [], []
    u_all, v_all = _cast_tables([peer_u, peer_v])

    for i in range(DEPTH):
        j = i // 2
        sh1, sc1, g1, sh2, sc2, g2 = (mods[i, m] for m in range(6))
        wq = peer_wq[i].astype(BF16)
        if i % 2 == 0:
            lam_init = 0.8 - 0.6 * math.exp(-0.3 * i)
            w_in = w_in_e[j].astype(BF16)
            w_out = w_out_e[j].astype(BF16)
            pw = pool_w[j].astype(BF16)
            lamv = jnp.zeros((8, LANES), F32)
            lamv = lamv.at[0, :A_HD].set(lam_q1[j]).at[1, :A_HD].set(lam_k1[j])
            lamv = lamv.at[2, :A_HD].set(lam_q2[j]).at[3, :A_HD].set(lam_k2[j])
        else:
            w1 = pw1_w[j].astype(BF16)
            w2 = pw2_w[j].astype(BF16)

        for si, st in enumerate(streams):
            x, mod_row = st["x"], st["mod_row"](TM)
            if i % 2 == 0:
                proj = _linear([x], w_in, mod_row=mod_row, tm=TM, tn=1024, prologue="rms_mod",
                               norm_g=norm1_g[i], shift=sh1, scale=sc1, name="w_in")
                proj3 = proj.reshape(st["b"], st["t"], IN_EVEN)
                if si == 0:
                    new_k.append(proj3[:, :, A_QK:2 * A_QK].reshape(bp, tp, A_HEADS, 2, A_HD))
                    new_v.append(proj3[:, :, 2 * A_QK:2 * A_QK + A_V].reshape(bp, tp, A_HEADS, 2 * A_HD))
                    att = _attention(proj3, lamv, subln_g[j], lam_init, tq=tp, heads_per_step=4)
                else:
                    ctx = (cache_k[:, j].reshape(bs, -1, A_QK), cache_v[:, j].reshape(bs, -1, A_V))
                    att = _attention(proj3, lamv, subln_g[j], lam_init, tq=1024, ctx=ctx, rope=rope)
                pooled = _pool(proj3, pw, pool_scale[j])
                x = _linear([att.reshape(-1, A_V), pooled.reshape(-1, POOL_WIDTH)], w_out,
                            mod_row=mod_row, tm=TM, tn=1024, epilogue="resid", res=x, gate=g1,
                            name="w_out")
            else:
                u = _linear([x], w1, mod_row=mod_row, tm=TM, tn=1024, prologue="rms_mod",
                            epilogue="glu", bias=pw1_b[j], norm_g=norm1_g[i], shift=sh1, scale=sc1,
                            out_dtype=BF16, name="pw1_glu")
                u = _dwconv(u.reshape(st["b"], st["t"], d), dw_w[j], dw_b[j]).reshape(-1, d)
                x = _linear([u], w2, mod_row=mod_row, tm=TM, tn=1024, prologue="ln_silu",
                            epilogue="resid", bias=pw2_b[j], norm_g=cln_g[j], norm_b=cln_b[j],
                            res=x, gate=g1, name="pw2")

            q, xt = _linear([x], wq, mod_row=mod_row, tm=TM, tn=1024, prologue="rms_mod",
                            norm_g=norm2_g[i], shift=sh2, scale=sc2, emit_xt=True, out_dtype=BF16,
                            name="peer_q")
            ab, thr = _peer_topk(q, peer_keys[i], tt=TOPK_TT)
            x = _peer_dense(xt, u_all, v_all, i, ab, thr, x, g2, mod_row=st["mod_row"](PEER_TT),
                            tt=PEER_TT, et=PEER_ET, out_norm_g=final_g if i == DEPTH - 1 else None)
            st["x"] = x

    y_prompt = streams[0]["x"].reshape(bp, tp, d)
    y_sample = streams[1]["x"].reshape(bs, ts, d)
    return (y_prompt, y_sample, jnp.stack(new_k, axis=1), jnp.stack(new_v, axis=1))
```
